```python
import jax
import jax.numpy as jnp
from jax import lax
import numpy as np

D_MODEL = 4096
BATCH = 8
SEQ = 2048
DEPTH = 1

GRID_W = 64
CTX_LEN = 256
EPS = 1e-6
N_MOD = 6
N_BRANCHES = 2

MLA_HEADS = 16
Q_LORA = 1024
KV_LORA = 512
QK_NOPE = 128
QK_ROPE = 64
V_HEAD = 128
Q_BLOCK = 128
ROPE_THETA = 10000.0

DN_HEADS = 16
DN_DK = 128
DN_DV = 128
DN_QK = DN_HEADS * DN_DK
DN_V = DN_HEADS * DN_DV
CONV_W = 5
CHUNK = 64

N_GROUPS = 4
EXPERTS_PER_GROUP = 8
N_EXPERTS = N_GROUPS * EXPERTS_PER_GROUP
TOP_K = 2
EXPERT_HIDDEN = 1024
MOE_BLOCK = 256

OFF_CKV = 0
OFF_KR = OFF_CKV + KV_LORA
OFF_AB = OFF_KR + QK_ROPE
OFF_DK = OFF_AB + 4 * DN_HEADS
OFF_DV = OFF_DK + DN_QK
OFF_DQ = OFF_DV + DN_V
OFF_CQ = OFF_DQ + DN_QK
OFF_Z = OFF_CQ + Q_LORA
OFF_GATE = OFF_Z + DN_V
IN_COLS = OFF_GATE + N_BRANCHES * D_MODEL
CTX_COLS = OFF_DQ

kernel_name = "hybrid_mla_gdn_hmoe_dit_block"


def rmsnorm(x, g):
    xf = x.astype(jnp.float32)
    y = xf * lax.rsqrt(jnp.mean(xf * xf, axis=-1, keepdims=True) + EPS)
    return (y * g).astype(x.dtype)


def l2norm(x):
    xf = x.astype(jnp.float32)
    return (xf * lax.rsqrt(jnp.sum(xf * xf, axis=-1, keepdims=True) + EPS)).astype(x.dtype)


def modulate(h, shift, scale):
    return h * (1.0 + scale) + shift


def axial_rope(n):
    rows = n // GRID_W
    row = jnp.repeat(jnp.arange(rows), GRID_W).astype(jnp.float32)
    col = jnp.tile(jnp.arange(GRID_W), rows).astype(jnp.float32)
    half = QK_ROPE // 2
    inv_freq = ROPE_THETA ** (-jnp.arange(0, half, 2, dtype=jnp.float32) / half)
    ang = jnp.concatenate([row[:, None] * inv_freq, col[:, None] * inv_freq], axis=-1)
    return jnp.cos(ang), jnp.sin(ang)


def apply_rope(x, cos, sin):
    x0, x1 = x[..., 0::2], x[..., 1::2]
    r0 = x0 * cos - x1 * sin
    r1 = x0 * sin + x1 * cos
    return jnp.stack([r0, r1], axis=-1).reshape(x.shape).astype(x.dtype)


def mla_keys(proj, p, rope):
    b, t, _ = proj.shape
    ckv = rmsnorm(proj[..., OFF_CKV:OFF_KR], p["kv_norm_g"])
    kv = (ckv @ p["w_ukv"]).reshape(b, t, MLA_HEADS, QK_NOPE + V_HEAD)
    k_nope = rmsnorm(kv[..., :QK_NOPE], p["k_norm_g"])
    v = kv[..., QK_NOPE:]
    k_rope = rmsnorm(proj[..., OFF_KR:OFF_AB], p["k_rope_norm_g"])
    if rope is not None:
        k_rope = apply_rope(k_rope, rope[0], rope[1])
    return k_nope, k_rope, v


def mla_queries(proj, p, rope):
    b, t, _ = proj.shape
    cq = rmsnorm(proj[..., OFF_CQ:OFF_Z], p["q_a_norm_g"])
    q = (cq @ p["w_uq"]).reshape(b, t, MLA_HEADS, QK_NOPE + QK_ROPE)
    q_nope = rmsnorm(q[..., :QK_NOPE], p["q_norm_g"])
    q_rope = rmsnorm(q[..., QK_NOPE:], p["q_rope_norm_g"])
    if rope is not None:
        q_rope = apply_rope(q_rope, rope[0][:, None, :], rope[1][:, None, :])
    return q_nope, q_rope


def mla_attend(q_nope, q_rope, k_nope, k_rope, v):
    b, t, h, _ = q_nope.shape
    nb = t // Q_BLOCK
    scale = (QK_NOPE + QK_ROPE) ** -0.5

    def blocks(z):
        return z.reshape(b, nb, Q_BLOCK, *z.shape[2:]).swapaxes(0, 1)

    def one_block(args):
        qn, qr = args
        s = jnp.einsum("bqhd,bkhd->bhqk", qn, k_nope) + jnp.einsum("bqhr,bkr->bhqk", qr, k_rope)
        prob = jax.nn.softmax(s.astype(jnp.float32) * scale, axis=-1).astype(v.dtype)
        return jnp.einsum("bhqk,bkhd->bqhd", prob, v)

    o = lax.map(one_block, (blocks(q_nope), blocks(q_rope)))
    return o.swapaxes(0, 1).reshape(b, t, h * V_HEAD)


def short_conv(x, w):
    y = lax.conv_general_dilated(
        x, w[:, None, :], window_strides=(1,), padding=[(CONV_W // 2, CONV_W // 2)],
        dimension_numbers=("NWC", "WIO", "NWC"), feature_group_count=x.shape[-1])
    return jax.nn.silu(y)


def delta_inputs(proj, p, with_q):
    b, t, _ = proj.shape
    width = DN_QK + DN_V + (DN_QK if with_q else 0)
    kvq = short_conv(proj[..., OFF_DK:OFF_DK + width], p["conv_w"][:, :width])
    k = l2norm(kvq[..., :DN_QK].reshape(b, t, DN_HEADS, DN_DK))
    v = kvq[..., DN_QK:DN_QK + DN_V].reshape(b, t, DN_HEADS, DN_DV)
    q = l2norm(kvq[..., DN_QK + DN_V:].reshape(b, t, DN_HEADS, DN_DK)) if with_q else None
    ab = proj[..., OFF_AB:OFF_DK].astype(jnp.float32).reshape(b, t, 2, 2, DN_HEADS)
    log_alpha = -jnp.exp(p["a_log"].astype(jnp.float32)) * jax.nn.softplus(ab[:, :, 0] + p["dt_bias"])
    beta = jax.nn.sigmoid(ab[:, :, 1])
    return q, k, v, log_alpha, beta


def gated_delta_chunked(q, k, v, log_alpha, beta, s0):
    b, t, h, dk = k.shape
    dv = v.shape[-1]
    n = t // CHUNK

    def to_chunks(z):
        z = z.astype(jnp.float32).reshape(b, n, CHUNK, h, *z.shape[3:])
        return jnp.moveaxis(z, (1, 3), (0, 2))

    kc, vc, bc = to_chunks(k), to_chunks(v), to_chunks(beta)
    gc = jnp.cumsum(to_chunks(log_alpha), axis=-1)
    idx = jnp.arange(CHUNK)
    incl = idx[:, None] >= idx[None, :]
    strict = idx[:, None] > idx[None, :]
    diff = gc[..., :, None] - gc[..., None, :]
    decay = jnp.where(incl, jnp.exp(jnp.where(incl, diff, 0.0)), 0.0)
    kb = kc * bc[..., None]
    lmat = jnp.where(strict, jnp.einsum("nbhcd,nbhsd->nbhcs", kb, kc) * decay, 0.0)
    rhs = jnp.concatenate([vc * bc[..., None], kb * jnp.exp(gc)[..., None]], axis=-1)
    sol = lax.linalg.triangular_solve(jnp.eye(CHUNK, dtype=jnp.float32) + lmat, rhs,
                                      left_side=True, lower=True, unit_diagonal=True)
    u, w = sol[..., :dv], sol[..., dv:]
    k_tail = kc * jnp.exp(gc[..., -1:] - gc)[..., None]
    g_last = jnp.exp(gc[..., -1])
    if q is None:
        xs = (u, w, k_tail, g_last)
    else:
        qc = to_chunks(q) * (dk ** -0.5)
        qk = jnp.einsum("nbhcd,nbhsd->nbhcs", qc, kc) * decay
        xs = (u, w, k_tail, g_last, qk, qc * jnp.exp(gc)[..., None])

    def step(state, xs_c):
        u_c, w_c, kt_c, gl_c = xs_c[:4]
        v_new = u_c - jnp.einsum("bhcd,bhde->bhce", w_c, state)
        new_state = state * gl_c[..., None, None] + jnp.einsum("bhcd,bhce->bhde", kt_c, v_new)
        if len(xs_c) == 4:
            return new_state, None
        qk_c, qg_c = xs_c[4:]
        o_c = jnp.einsum("bhcd,bhde->bhce", qg_c, state) + jnp.einsum("bhcs,bhse->bhce", qk_c, v_new)
        return new_state, o_c

    s_final, o = lax.scan(step, s0.astype(jnp.float32), xs)
    if q is None:
        return None, s_final
    o = jnp.moveaxis(o, (0, 2), (1, 3)).reshape(b, t, h, dv)
    return o.astype(v.dtype), s_final


def bidir_delta(q, k, v, log_alpha, beta, s0):
    def flip(z):
        return None if z is None else jnp.flip(z, axis=1)
    o_f, s_f = gated_delta_chunked(q, k, v, log_alpha[:, :, 0], beta[:, :, 0], s0[0])
    o_b, s_b = gated_delta_chunked(flip(q), flip(k), flip(v), flip(log_alpha[:, :, 1]),
                                   flip(beta[:, :, 1]), s0[1])
    o = None if q is None else o_f + flip(o_b)
    return o, (s_f, s_b)


def merge_branches(proj, att, od, p):
    b, t, _ = proj.shape
    o_a = att @ p["w_oa"]
    z = jax.nn.silu(proj[..., OFF_Z:OFF_GATE]).reshape(b, t, DN_HEADS, DN_DV)
    o_b = (rmsnorm(od, p["dn_norm_g"]) * z).reshape(b, t, DN_V) @ p["w_ob"]
    g = jax.nn.sigmoid(proj[..., OFF_GATE:].reshape(b, t, N_BRANCHES, D_MODEL))
    return (g[:, :, 0] * o_a + g[:, :, 1] * o_b) @ p["w_out"]


def moe_ffn(h, p):
    n, d = h.shape
    hf = h.astype(jnp.float32)
    pg = jax.nn.softmax(hf @ p["w_rg"] + p["b_rg"], axis=-1)
    pg_top, g_top = lax.top_k(pg, 1)
    le = (hf @ p["w_re"] + p["b_re"]).reshape(n, N_GROUPS, EXPERTS_PER_GROUP)
    le_g = le[jnp.arange(n), g_top[:, 0]]
    pe_top, e_top = lax.top_k(jax.nn.softmax(le_g, axis=-1), TOP_K)
    gate = pe_top / jnp.sum(pe_top, axis=-1, keepdims=True) * pg_top
    expert = g_top * EXPERTS_PER_GROUP + e_top

    a = n * TOP_K
    e_flat = expert.reshape(a)
    order = jnp.argsort(e_flat)
    e_s = e_flat[order]
    t_s = order // TOP_K
    p_s = gate.reshape(a)[order]
    counts = jax.ops.segment_sum(jnp.ones((a,), jnp.int32), e_flat, num_segments=N_EXPERTS)
    padded = (counts + MOE_BLOCK - 1) // MOE_BLOCK * MOE_BLOCK
    pad_end = jnp.cumsum(padded)
    pad_start = pad_end - padded
    seg_start = jnp.cumsum(counts) - counts
    dest = pad_start[e_s] + jnp.arange(a) - seg_start[e_s]
    n_blocks = -(-a // MOE_BLOCK) + N_EXPERTS
    x_pad = jnp.zeros((n_blocks * MOE_BLOCK, d), h.dtype).at[dest].set(h[t_s])
    blk_expert = jnp.minimum(
        jnp.searchsorted(pad_end, jnp.arange(n_blocks) * MOE_BLOCK, side="right"), N_EXPERTS - 1)

    def expert_block(args):
        xb, e = args
        hb = jax.nn.silu(xb @ p["w1"][e]) * (xb @ p["w3"][e])
        return hb @ p["w2"][e]

    y_pad = lax.map(expert_block, (x_pad.reshape(n_blocks, MOE_BLOCK, d), blk_expert)).reshape(-1, d)
    y = y_pad[dest] * p_s[:, None].astype(h.dtype)
    return jax.ops.segment_sum(y, t_s, num_segments=n)


def trunk_layer(xl, xc, mod_l, mod_c, rope, p, update_ctx):
    b, t, d = xl.shape
    s1, sc1, g1, s2, sc2, g2 = (mod_l[:, i, None, :] for i in range(N_MOD))
    cs1, csc1, cg1, cs2, csc2, cg2 = (mod_c[i] for i in range(N_MOD))
    hl = modulate(rmsnorm(xl, p["norm1_g"]), s1, sc1)
    hc = modulate(rmsnorm(xc, p["norm1_g"]), cs1, csc1)
    pl = hl @ p["w_in"]
    pc = hc @ (p["w_in"] if update_ctx else p["w_in"][:, :CTX_COLS])

    kn_c, kr_c, v_c = mla_keys(pc, p, None)
    qd_c, kd_c, vd_c, la_c, be_c = delta_inputs(pc, p, with_q=update_ctx)
    s_zero = jnp.zeros((xc.shape[0], DN_HEADS, DN_DK, DN_DV), jnp.float32)
    od_c, s_ctx = bidir_delta(qd_c, kd_c, vd_c, la_c, be_c, (s_zero, s_zero))

    kn_l, kr_l, v_l = mla_keys(pl, p, rope)
    qn_l, qr_l = mla_queries(pl, p, rope)
    att_l = mla_attend(qn_l, qr_l, jnp.concatenate([kn_l, kn_c], axis=1),
                       jnp.concatenate([kr_l, kr_c], axis=1), jnp.concatenate([v_l, v_c], axis=1))
    qd_l, kd_l, vd_l, la_l, be_l = delta_inputs(pl, p, with_q=True)
    od_l, _ = bidir_delta(qd_l, kd_l, vd_l, la_l, be_l, s_ctx)
    xl = xl + g1 * merge_branches(pl, att_l, od_l, p)
    hl2 = modulate(rmsnorm(xl, p["norm2_g"]), s2, sc2)

    if update_ctx:
        qn_c, qr_c = mla_queries(pc, p, None)
        att_c = mla_attend(qn_c, qr_c, kn_c, kr_c, v_c)
        xc = xc + cg1 * merge_branches(pc, att_c, od_c, p)
        hc2 = modulate(rmsnorm(xc, p["norm2_g"]), cs2, csc2)
        ffn = moe_ffn(jnp.concatenate([hl2.reshape(-1, d), hc2.reshape(-1, d)], axis=0), p)
        xl = xl + g2 * ffn[:b * t].reshape(b, t, d)
        xc = xc + cg2 * ffn[b * t:].reshape(xc.shape)
    else:
        xl = xl + g2 * moe_ffn(hl2.reshape(-1, d), p).reshape(b, t, d)
    return xl, xc


def setup_inputs(seed: int = 0) -> dict:
    key = jax.random.key(seed)
    ks = jax.random.split(key, 32)
    f32 = jnp.float32
    L, D = DEPTH, D_MODEL

    def nrm(k, shape, scale):
        return jax.random.normal(k, shape, f32) * scale

    def gain(k, shape):
        return 1.0 + 0.05 * jax.random.normal(k, shape, f32)

    dt = jnp.exp(jax.random.uniform(ks[19], (L, 2, DN_HEADS), f32, np.log(1e-3), np.log(1e-1)))
    return {
        "x": nrm(ks[0], (BATCH, SEQ, D), 1.0),
        "c": nrm(ks[1], (BATCH, D), 1.0),
        "ctx": nrm(ks[2], (BATCH, CTX_LEN, D), 1.0),
        "c_ctx": nrm(ks[3], (D,), 1.0),
        "norm1_g": gain(ks[4], (L, D)),
        "norm2_g": gain(ks[5], (L, D)),
        "w_mod": nrm(ks[6], (L, D, N_MOD * D), 0.5 * D ** -0.5),
        "b_mod": nrm(ks[7], (L, N_MOD * D), 0.02),
        "w_in": nrm(ks[8], (L, D, IN_COLS), D ** -0.5),
        "q_a_norm_g": gain(ks[9], (L, Q_LORA)),
        "w_uq": nrm(ks[10], (L, Q_LORA, MLA_HEADS * (QK_NOPE + QK_ROPE)), Q_LORA ** -0.5),
        "kv_norm_g": gain(ks[11], (L, KV_LORA)),
        "w_ukv": nrm(ks[12], (L, KV_LORA, MLA_HEADS * (QK_NOPE + V_HEAD)), KV_LORA ** -0.5),
        "q_norm_g": gain(ks[13], (L, QK_NOPE)),
        "q_rope_norm_g": gain(ks[14], (L, QK_ROPE)),
        "k_norm_g": gain(ks[15], (L, QK_NOPE)),
        "k_rope_norm_g": gain(ks[16], (L, QK_ROPE)),
        "conv_w": nrm(ks[17], (L, CONV_W, DN_QK + DN_V + DN_QK), CONV_W ** -0.5),
        "a_log": jnp.log(jax.random.uniform(ks[18], (L, 2, DN_HEADS), f32, 1.0, 16.0)),
        "dt_bias": dt + jnp.log(-jnp.expm1(-dt)),
        "dn_norm_g": gain(ks[20], (L, DN_DV)),
        "w_oa": nrm(ks[21], (L, MLA_HEADS * V_HEAD, D), (MLA_HEADS * V_HEAD) ** -0.5),
        "w_ob": nrm(ks[22], (L, DN_V, D), DN_V ** -0.5),
        "w_out": nrm(ks[23], (L, D, D), D ** -0.5),
        "w_rg": nrm(ks[24], (L, D, N_GROUPS), D ** -0.5),
        "b_rg": nrm(ks[25], (L, N_GROUPS), 0.01),
        "w_re": nrm(ks[26], (L, D, N_EXPERTS), D ** -0.5),
        "b_re": nrm(ks[27], (L, N_EXPERTS), 0.01),
        "w1": nrm(ks[28], (L, N_EXPERTS, D, EXPERT_HIDDEN), D ** -0.5),
        "w3": nrm(ks[29], (L, N_EXPERTS, D, EXPERT_HIDDEN), D ** -0.5),
        "w2": nrm(ks[30], (L, N_EXPERTS, EXPERT_HIDDEN, D), EXPERT_HIDDEN ** -0.5),
    }


def reference(x, c, ctx, c_ctx, norm1_g, norm2_g, w_mod, b_mod, w_in, q_a_norm_g, w_uq,
              kv_norm_g, w_ukv, q_norm_g, q_rope_norm_g, k_norm_g, k_rope_norm_g, conv_w,
              a_log, dt_bias, dn_norm_g, w_oa, w_ob, w_out, w_rg, b_rg, w_re, b_re, w1, w3, w2):
    rope = axial_rope(x.shape[1])
    xl, xc = x, ctx
    for layer in range(DEPTH):
        p = {
            "norm1_g": norm1_g[layer], "norm2_g": norm2_g[layer], "w_in": w_in[layer],
            "q_a_norm_g": q_a_norm_g[layer], "w_uq": w_uq[layer],
            "kv_norm_g": kv_norm_g[layer], "w_ukv": w_ukv[layer],
            "q_norm_g": q_norm_g[layer], "q_rope_norm_g": q_rope_norm_g[layer],
            "k_norm_g": k_norm_g[layer], "k_rope_norm_g": k_rope_norm_g[layer],
            "conv_w": conv_w[layer], "a_log": a_log[layer], "dt_bias": dt_bias[layer],
            "dn_norm_g": dn_norm_g[layer], "w_oa": w_oa[layer], "w_ob": w_ob[layer],
            "w_out": w_out[layer], "w_rg": w_rg[layer], "b_rg": b_rg[layer],
            "w_re": w_re[layer], "b_re": b_re[layer],
            "w1": w1[layer], "w3": w3[layer], "w2": w2[layer],
        }
        mod_l = (jax.nn.silu(c) @ w_mod[layer] + b_mod[layer]).reshape(x.shape[0], N_MOD, D_MODEL)
        mod_c = (jax.nn.silu(c_ctx) @ w_mod[layer] + b_mod[layer]).reshape(N_MOD, D_MODEL)
        xl, xc = trunk_layer(xl, xc, mod_l, mod_c, rope, p, update_ctx=layer < DEPTH - 1)
    return xl
```

```python
import functools

import numpy as np
import jax
import jax.numpy as jnp
from jax import lax
from jax.experimental import pallas as pl
from jax.experimental.pallas import tpu as pltpu

F32 = jnp.float32
BF16 = jnp.bfloat16

EPS = 1e-6
GRID_W = 64
N_MOD = 6
MLA_HEADS = 16
Q_LORA = 1024
KV_LORA = 512
QK_NOPE = 128
QK_ROPE = 64
V_HEAD = 128
ROPE_THETA = 10000.0
DN_HEADS = 16
DN_DK = 128
DN_DV = 128
DN_QK = DN_HEADS * DN_DK
DN_V = DN_HEADS * DN_DV
CONV_W = 5
N_GROUPS = 4
EXPERTS_PER_GROUP = 8
N_EXPERTS = N_GROUPS * EXPERTS_PER_GROUP
TOP_K = 2
EXPERT_HIDDEN = 1024

OFF_CKV = 0
OFF_KR = OFF_CKV + KV_LORA
OFF_AB = OFF_KR + QK_ROPE
OFF_DK = OFF_AB + 4 * DN_HEADS
OFF_DV = OFF_DK + DN_QK
OFF_DQ = OFF_DV + DN_V
OFF_CQ = OFF_DQ + DN_QK
OFF_Z = OFF_CQ + Q_LORA
OFF_GATE = OFF_Z + DN_V

V7X_LANES = 128
V7X_VMEM_BYTES = 64 * 1024 * 1024
VMEM_LIMIT = V7X_VMEM_BYTES - 8 * 1024 * 1024

DN_CHUNK = 128
MOE_ROWS = 256
HEAD_GROUP = 4
DN_HEADS_PER_STEP = 2


def _cparams(n_grid):
    return pltpu.CompilerParams(dimension_semantics=("arbitrary",) * n_grid, vmem_limit_bytes=VMEM_LIMIT)


def _pick_tile(n, target, mult=V7X_LANES):
    if n <= target:
        return n
    best = None
    t = mult
    while t <= target:
        if n % t == 0:
            best = t
        t += mult
    assert best is not None, (n, target)
    return best


def _rms(x, g):
    return x * lax.rsqrt(jnp.mean(x * x, axis=-1, keepdims=True) + EPS) * g


def _silu(x):
    return x * jax.nn.sigmoid(x)


def _mm_body(*refs, pre, epi, n_extra, cached):
    x_ref, w_ref = refs[0], refs[1]
    extra = refs[2:2 + n_extra]
    o_ref = refs[2 + n_extra]
    if cached:
        xs_ref = refs[3 + n_extra]

        @pl.when(pl.program_id(2) == 0)
        def _():
            xs_ref[...] = pre(x_ref[0]).astype(BF16)

        x = xs_ref[...]
    else:
        x = x_ref[0]
    w = w_ref[...]
    if w.dtype != BF16:
        w = w.astype(BF16)
    acc = jnp.dot(x, w, preferred_element_type=F32)
    if epi is not None:
        acc = epi(acc, *[e[0] for e in extra])
    o_ref[0] = acc.astype(o_ref.dtype)


def _matmul(x, w, *, rows=None, tm=1024, tn=512, out_dtype=BF16, pre=None, epi=None, extras=(), name="mm"):
    B, S, K = x.shape
    N = w.shape[1]
    rows = S if rows is None else rows
    tm = _pick_tile(rows, tm)
    tn = _pick_tile(N, tn)
    grid = (B, rows // tm, N // tn)
    in_specs = [pl.BlockSpec((1, tm, K), lambda b, i, j: (b, i, 0)),
                pl.BlockSpec((K, tn), lambda b, i, j: (0, j))]
    args = [x, w]
    for arr, kind in extras:
        if kind == "tile":
            in_specs.append(pl.BlockSpec((1, tm, tn), lambda b, i, j: (b, i, j)))
        elif kind == "bcol":
            if arr.shape[0] == 1:
                in_specs.append(pl.BlockSpec((1, 1, tn), lambda b, i, j: (0, 0, j)))
            else:
                in_specs.append(pl.BlockSpec((1, 1, tn), lambda b, i, j: (b, 0, j)))
        else:
            raise ValueError(kind)
        args.append(arr)
    cached = pre is not None
    scratch = [pltpu.VMEM((tm, K), BF16)] if cached else []
    if not cached:
        assert x.dtype == BF16
    return pl.pallas_call(
        functools.partial(_mm_body, pre=pre, epi=epi, n_extra=len(extras), cached=cached),
        out_shape=jax.ShapeDtypeStruct((B, rows, N), out_dtype),
        grid=grid, in_specs=in_specs,
        out_specs=pl.BlockSpec((1, tm, tn), lambda b, i, j: (b, i, j)),
        scratch_shapes=scratch, compiler_params=_cparams(3), name=name,
    )(*args)


def _norm_mod_body(x_ref, c_ref, g_ref, sl_ref, scl_ref, sc_ref, scc_ref, o_ref, *, n_lat):
    j = pl.program_id(1)

    @pl.when(j < n_lat)
    def _():
        h = _rms(x_ref[0], g_ref[...])
        o_ref[0] = (h * (1.0 + scl_ref[0]) + sl_ref[0]).astype(o_ref.dtype)

    @pl.when(j >= n_lat)
    def _():
        h = _rms(c_ref[0], g_ref[...])
        o_ref[0] = (h * (1.0 + scc_ref[0]) + sc_ref[0]).astype(o_ref.dtype)


def _norm_mod(x, ctx, g, shift_l, scale_l, shift_c, scale_c):
    B, T, D = x.shape
    Tc = ctx.shape[1]
    tr = _pick_tile(int(np.gcd(T, Tc)), 256, 8)
    n_lat, n_ctx = T // tr, Tc // tr
    return pl.pallas_call(
        functools.partial(_norm_mod_body, n_lat=n_lat),
        out_shape=jax.ShapeDtypeStruct((B, T + Tc, D), BF16),
        grid=(B, n_lat + n_ctx),
        in_specs=[pl.BlockSpec((1, tr, D), lambda b, j: (b, jnp.minimum(j, n_lat - 1), 0)),
                  pl.BlockSpec((1, tr, D), lambda b, j: (b, jnp.maximum(j - n_lat, 0), 0)),
                  pl.BlockSpec((1, D), lambda b, j: (0, 0)),
                  pl.BlockSpec((1, 1, D), lambda b, j: (b, 0, 0)),
                  pl.BlockSpec((1, 1, D), lambda b, j: (b, 0, 0)),
                  pl.BlockSpec((1, 1, D), lambda b, j: (0, 0, 0)),
                  pl.BlockSpec((1, 1, D), lambda b, j: (0, 0, 0))],
        out_specs=pl.BlockSpec((1, tr, D), lambda b, j: (b, j, 0)),
        compiler_params=_cparams(2), name="norm_mod",
    )(x, ctx, g, shift_l, scale_l, shift_c, scale_c)


def _mla_small_body(p_ref, tab_ref, gkv_ref, gkr_ref, gks_ref, ckv_ref, kr_ref):
    p = p_ref[0]
    ckv_ref[0] = _rms(p[:, :KV_LORA], gkv_ref[...]).astype(ckv_ref.dtype)
    xr = p[:, KV_LORA:KV_LORA + 128]
    xs = p[:, KV_LORA + 128:KV_LORA + 256]
    inv = lax.rsqrt(jnp.sum(xr * xr, axis=-1, keepdims=True) * (1.0 / QK_ROPE) + EPS)
    tab = tab_ref[...]
    kr = inv * (xr * gkr_ref[...] * tab[:, :128] + xs * gks_ref[...] * tab[:, 128:])
    kr_ref[0] = kr.astype(kr_ref.dtype)


def _mla_small(p_small, tab, g_kv, g_kr, g_ks):
    B, S, W = p_small.shape
    tr = _pick_tile(S, 768, 8)
    return pl.pallas_call(
        _mla_small_body,
        out_shape=(jax.ShapeDtypeStruct((B, S, KV_LORA), BF16), jax.ShapeDtypeStruct((B, S, 128), BF16)),
        grid=(B, S // tr),
        in_specs=[pl.BlockSpec((1, tr, 768), lambda b, i: (b, i, 0)),
                  pl.BlockSpec((tr, 256), lambda b, i: (i, 0)),
                  pl.BlockSpec((1, KV_LORA), lambda b, i: (0, 0)),
                  pl.BlockSpec((1, 128), lambda b, i: (0, 0)),
                  pl.BlockSpec((1, 128), lambda b, i: (0, 0))],
        out_specs=(pl.BlockSpec((1, tr, KV_LORA), lambda b, i: (b, i, 0)),
                   pl.BlockSpec((1, tr, 128), lambda b, i: (b, i, 0))),
        compiler_params=_cparams(2), name="mla_small",
    )(p_small, tab, g_kv, g_kr, g_ks)


def _kv_up_body(x_ref, w_ref, kr_ref, gk_ref, k_ref, v_ref, *, hg):
    acc = jnp.dot(x_ref[0], w_ref[...], preferred_element_type=F32)
    kr = kr_ref[0]
    for h in range(hg):
        kn = _rms(acc[:, h * 128:(h + 1) * 128], gk_ref[...])
        k_ref[0, :, h * 256:h * 256 + 128] = kn.astype(k_ref.dtype)
        k_ref[0, :, h * 256 + 128:(h + 1) * 256] = kr
    v_ref[0] = acc[:, hg * 128:].astype(v_ref.dtype)


def _kv_up(ckv, w_perm, kr, g_k):
    B, S, _ = ckv.shape
    hg = HEAD_GROUP
    tm = _pick_tile(S, 1152, 8)
    return pl.pallas_call(
        functools.partial(_kv_up_body, hg=hg),
        out_shape=(jax.ShapeDtypeStruct((B, S, MLA_HEADS * 256), BF16),
                   jax.ShapeDtypeStruct((B, S, MLA_HEADS * V_HEAD), BF16)),
        grid=(B, S // tm, MLA_HEADS // hg),
        in_specs=[pl.BlockSpec((1, tm, KV_LORA), lambda b, i, j: (b, i, 0)),
                  pl.BlockSpec((KV_LORA, hg * 256), lambda b, i, j: (0, j)),
                  pl.BlockSpec((1, tm, 128), lambda b, i, j: (b, i, 0)),
                  pl.BlockSpec((1, 128), lambda b, i, j: (0, 0))],
        out_specs=(pl.BlockSpec((1, tm, hg * 256), lambda b, i, j: (b, i, j)),
                   pl.BlockSpec((1, tm, hg * 128), lambda b, i, j: (b, i, j))),
        compiler_params=_cparams(3), name="kv_up",
    )(ckv, w_perm, kr, g_k)


def _q_up_body(x_ref, w_ref, tab_ref, gq_ref, gqa_ref, gr_ref, gs_ref, q_ref, xs_ref, *, hg, scale):
    @pl.when(pl.program_id(2) == 0)
    def _():
        xs_ref[...] = _rms(x_ref[0], gqa_ref[...]).astype(BF16)

    acc = jnp.dot(xs_ref[...], w_ref[...], preferred_element_type=F32)
    tab = tab_ref[...]
    for h in range(hg):
        qn = _rms(acc[:, h * 128:(h + 1) * 128], gq_ref[...]) * scale
        xr = acc[:, (hg + h) * 128:(hg + h + 1) * 128]
        xw = acc[:, (2 * hg + h) * 128:(2 * hg + h + 1) * 128]
        inv = lax.rsqrt(jnp.sum(xr * xr, axis=-1, keepdims=True) * (1.0 / QK_ROPE) + EPS) * scale
        qr = inv * (xr * gr_ref[...] * tab[:, :128] + xw * gs_ref[...] * tab[:, 128:])
        q_ref[0, :, h * 256:h * 256 + 128] = qn.astype(q_ref.dtype)
        q_ref[0, :, h * 256 + 128:(h + 1) * 256] = qr.astype(q_ref.dtype)


def _q_up(cq, w_perm, tab, g_q, g_qa, g_r, g_s):
    B, T, _ = cq.shape
    hg = HEAD_GROUP
    tm = _pick_tile(T, 512, 8)
    scale = float((QK_NOPE + QK_ROPE) ** -0.5)
    return pl.pallas_call(
        functools.partial(_q_up_body, hg=hg, scale=scale),
        out_shape=jax.ShapeDtypeStruct((B, T, MLA_HEADS * 256), BF16),
        grid=(B, T // tm, MLA_HEADS // hg),
        in_specs=[pl.BlockSpec((1, tm, Q_LORA), lambda b, i, j: (b, i, 0)),
                  pl.BlockSpec((Q_LORA, hg * 384), lambda b, i, j: (0, j)),
                  pl.BlockSpec((tm, 256), lambda b, i, j: (i, 0)),
                  pl.BlockSpec((1, 128), lambda b, i, j: (0, 0)),
                  pl.BlockSpec((1, Q_LORA), lambda b, i, j: (0, 0)),
                  pl.BlockSpec((1, 128), lambda b, i, j: (0, 0)),
                  pl.BlockSpec((1, 128), lambda b, i, j: (0, 0))],
        out_specs=pl.BlockSpec((1, tm, hg * 256), lambda b, i, j: (b, i, j)),
        scratch_shapes=[pltpu.VMEM((tm, Q_LORA), BF16)],
        compiler_params=_cparams(3), name="q_up",
    )(cq, w_perm, tab, g_q, g_qa, g_r, g_s)


def _attn_body(q_ref, k_ref, v_ref, o_ref):
    s = lax.dot_general(q_ref[0], k_ref[0], (((1,), (1,)), ((), ())), preferred_element_type=F32)
    m = jnp.max(s, axis=-1, keepdims=True)
    p = jnp.exp(s - m)
    l = jnp.sum(p, axis=-1, keepdims=True)
    o = jnp.dot(p.astype(BF16), v_ref[0], preferred_element_type=F32)
    o_ref[0] = (o / l).astype(o_ref.dtype)


def _attention(q, k, v):
    B, T, _ = q.shape
    S = k.shape[1]
    tq = _pick_tile(T, 512, 8)
    return pl.pallas_call(
        _attn_body,
        out_shape=jax.ShapeDtypeStruct((B, T, MLA_HEADS * V_HEAD), BF16),
        grid=(B, MLA_HEADS, T // tq),
        in_specs=[pl.BlockSpec((1, tq, 256), lambda b, h, i: (b, i, h)),
                  pl.BlockSpec((1, S, 256), lambda b, h, i: (b, 0, h)),
                  pl.BlockSpec((1, S, V_HEAD), lambda b, h, i: (b, 0, h))],
        out_specs=pl.BlockSpec((1, tq, V_HEAD), lambda b, h, i: (b, i, h)),
        compiler_params=_cparams(3), name="mla_attention",
    )(q, k, v)


def _conv_body(prev_ref, cur_ref, next_ref, w_ref, o_ref, buf_ref, *, n_lat, n_tiles, l2, scale, tr):
    j = pl.program_id(1)
    tc = cur_ref.shape[2]
    prev_ok = jnp.logical_and(j != 0, j != n_lat)
    next_ok = jnp.logical_and(j != n_lat - 1, j != n_tiles - 1)
    zeros8 = jnp.zeros((8, tc), F32)
    buf_ref[0:8, :] = jnp.where(prev_ok, prev_ref[0, tr - 16:tr, :].astype(F32)[8:], zeros8)
    buf_ref[8:8 + tr, :] = cur_ref[0].astype(F32)
    buf_ref[8 + tr:16 + tr, :] = jnp.where(next_ok, next_ref[0, 0:16, :].astype(F32)[:8], zeros8)
    w = w_ref[...]
    y = jnp.zeros((tr, tc), F32)
    for t in range(CONV_W):
        r0 = 8 + t - CONV_W // 2
        y = y + buf_ref[r0:r0 + tr, :] * w[t:t + 1, :]
    y = _silu(y)
    if l2:
        for h in range(tc // 128):
            yh = y[:, h * 128:(h + 1) * 128]
            yh = yh * lax.rsqrt(jnp.sum(yh * yh, axis=-1, keepdims=True) + EPS) * scale
            o_ref[0, :, h * 128:(h + 1) * 128] = yh.astype(o_ref.dtype)
    else:
        o_ref[0] = y.astype(o_ref.dtype)


def _short_conv(x, col0, width, w8, *, n_lat_rows, l2, scale=1.0):
    B, R, _ = x.shape
    tr = _pick_tile(int(np.gcd(n_lat_rows, R - n_lat_rows)) if R > n_lat_rows else n_lat_rows, 256, 8)
    tc = _pick_tile(width, 512)
    n_tiles, n_lat = R // tr, n_lat_rows // tr
    cb = col0 // tc
    assert col0 % tc == 0
    return pl.pallas_call(
        functools.partial(_conv_body, n_lat=n_lat, n_tiles=n_tiles, l2=l2, scale=scale, tr=tr),
        out_shape=jax.ShapeDtypeStruct((B, R, width), BF16),
        grid=(B, n_tiles, width // tc),
        in_specs=[pl.BlockSpec((1, tr, tc), lambda b, j, c: (b, jnp.maximum(j - 1, 0), cb + c)),
                  pl.BlockSpec((1, tr, tc), lambda b, j, c: (b, j, cb + c)),
                  pl.BlockSpec((1, tr, tc), lambda b, j, c: (b, jnp.minimum(j + 1, n_tiles - 1), cb + c)),
                  pl.BlockSpec((8, tc), lambda b, j, c: (0, c))],
        out_specs=pl.BlockSpec((1, tr, tc), lambda b, j, c: (b, j, c)),
        scratch_shapes=[pltpu.VMEM((tr + 16, tc), F32)],
        compiler_params=_cparams(3), name="short_conv",
    )(x, x, x, w8)


def _dn_consts():
    C = DN_CHUNK
    r = np.arange(C)[:, None]
    c = np.arange(C)[None, :]
    tri = np.stack([(c <= r), (c >= r)]).astype(np.float32)
    ones = np.ones((C, C), np.float32)
    rhs = np.stack([np.concatenate([(r > c).astype(np.float32), ones], 1),
                    np.concatenate([(r < c).astype(np.float32), ones], 1)])
    incl = np.stack([(r >= c), (r <= c)]).astype(np.float32)
    strict = np.stack([(r > c), (r < c)]).astype(np.float32)
    levels = []
    s = 1
    while s < C:
        levels.append(((r // (2 * s) == c // (2 * s)) & (r // s != c // s)).astype(np.float32))
        s *= 2
    return tri, rhs, incl, strict, np.stack(levels)


def _dn_chunk(k, kT, v, q, a_row, b_row, a_scale, dt_b, S, consts, d, with_q):
    C = DN_CHUNK
    tri_ref, rhs_ref, incl_ref, strict_ref, lev_ref = consts
    la = -a_scale * jax.nn.softplus(a_row + dt_b)
    beta = jax.nn.sigmoid(b_row)
    dg = jnp.dot(tri_ref[d] * la, rhs_ref[d], precision=lax.Precision.HIGHEST, preferred_element_type=F32)
    diff, gcol = dg[:, :C], dg[:, C:]
    last = C - 1 if d == 0 else 0
    tail_row = jnp.exp(diff[last:last + 1, :])
    g_last = jnp.exp(gcol[last:last + 1, :])
    dec = incl_ref[d] * jnp.exp(diff * incl_ref[d])
    egc = jnp.exp(gcol)
    gram = jnp.dot(k, kT, preferred_element_type=F32)
    nd = strict_ref[d] * gram * dec * beta
    eye = incl_ref[0] * incl_ref[1]
    x = eye - nd * lev_ref[0]
    for lv in range(1, lev_ref.shape[0]):
        m = (nd * lev_ref[lv]).astype(BF16)
        xm = jnp.dot(x.astype(BF16), m, preferred_element_type=F32)
        x = x - jnp.dot(xm.astype(BF16), x.astype(BF16), preferred_element_type=F32)
    ke = (k.astype(F32) * egc).astype(BF16)
    y = jnp.dot(x.astype(BF16), jnp.concatenate([v, ke], axis=1), preferred_element_type=F32)
    yv, yk = y[:, :DN_DV], y[:, DN_DV:]
    s_b = S.astype(BF16)
    if with_q:
        qg = (q.astype(F32) * egc).astype(BF16)
        r2 = jnp.dot(jnp.concatenate([yk.astype(BF16), qg], axis=0), s_b, preferred_element_type=F32)
        yks, qs = r2[:C], r2[C:]
    else:
        yks = jnp.dot(yk.astype(BF16), s_b, preferred_element_type=F32)
    vt = (yv - yks).astype(BF16)
    kt_tail = (kT.astype(F32) * (beta * tail_row)).astype(BF16)
    s_new = S * g_last + jnp.dot(kt_tail, vt, preferred_element_type=F32)
    if not with_q:
        return s_new, None
    qk = jnp.dot(q, kT, preferred_element_type=F32)
    p = (qk * dec * beta).astype(BF16)
    o = qs + jnp.dot(p, vt, preferred_element_type=F32)
    return s_new, o


def _dn_body(alog_ref, dtb_ref, k_ref, kT_ref, v_ref, q_ref, ab_ref, z_ref, g_ref,
             tri_ref, rhs_ref, incl_ref, strict_ref, lev_ref, o_ref, acc_ref, *, hb, n_lat, n_ctx):
    C = DN_CHUNK
    hg = pl.program_id(1)
    consts = (tri_ref, rhs_ref, incl_ref, strict_ref, lev_ref)
    acc_ref[...] = jnp.zeros(acc_ref.shape, F32)

    def run_chunk(S_all, cf, cb, with_q):
        new = []
        for h in range(hb):
            head = hg * hb + h
            for d, c in ((0, cf), (1, cb)):
                row0 = pl.multiple_of(c * C, C)
                k = k_ref[0, pl.ds(row0, C), h * 128:(h + 1) * 128]
                v = v_ref[0, pl.ds(row0, C), h * 128:(h + 1) * 128]
                kT = kT_ref[0, h, c]
                q = q_ref[0, pl.ds(row0, C), h * 128:(h + 1) * 128] if with_q else None
                a_row = ab_ref[0, h, d, pl.ds(c, 1), :]
                b_row = ab_ref[0, h, 2 + d, pl.ds(c, 1), :]
                a_scale = jnp.exp(alog_ref[d, head])
                s_new, o = _dn_chunk(k, kT, v, q, a_row, b_row, a_scale, dtb_ref[d, head], S_all[2 * h + d],
                                     consts, d, with_q)
                new.append(s_new)
                if with_q:
                    acc_ref[pl.ds(row0, C), h * 128:(h + 1) * 128] += o
        return tuple(new)

    S0 = tuple(jnp.zeros((DN_DK, DN_DV), F32) for _ in range(2 * hb))
    S1 = lax.fori_loop(0, n_ctx, lambda n, S: run_chunk(S, n_lat + n, n_lat + n_ctx - 1 - n, False), S0)
    lax.fori_loop(0, n_lat, lambda n, S: run_chunk(S, n, n_lat - 1 - n, True), S1)

    rows = 256 if (n_lat * C) % 256 == 0 else C

    def fin(i, carry):
        r0 = pl.multiple_of(i * rows, rows)
        for h in range(hb):
            od = acc_ref[pl.ds(r0, rows), h * 128:(h + 1) * 128]
            zz = z_ref[0, pl.ds(r0, rows), h * 128:(h + 1) * 128].astype(F32)
            o_ref[0, pl.ds(r0, rows), h * 128:(h + 1) * 128] = (_rms(od, g_ref[...]) * zz).astype(o_ref.dtype)
        return carry

    lax.fori_loop(0, (n_lat * C) // rows, fin, 0)


def _deltanet(k, kT, v, q, ab, z, a_log, dt_bias, g_dn):
    B, S, _ = k.shape
    T = q.shape[1]
    C = DN_CHUNK
    hb = DN_HEADS_PER_STEP
    n_lat, n_ctx = T // C, (S - T) // C
    consts = [jnp.asarray(a) for a in _dn_consts()]
    smem = pl.BlockSpec(memory_space=pltpu.SMEM)

    def full(a):
        nd = a.ndim
        return pl.BlockSpec(a.shape, lambda b, g, _n=nd: (0,) * _n)

    return pl.pallas_call(
        functools.partial(_dn_body, hb=hb, n_lat=n_lat, n_ctx=n_ctx),
        out_shape=jax.ShapeDtypeStruct((B, T, DN_V), BF16),
        grid=(B, DN_HEADS // hb),
        in_specs=[smem, smem,
                  pl.BlockSpec((1, S, hb * 128), lambda b, g: (b, 0, g)),
                  pl.BlockSpec((1, hb, S // C, 128, C), lambda b, g: (b, g, 0, 0, 0)),
                  pl.BlockSpec((1, S, hb * 128), lambda b, g: (b, 0, g)),
                  pl.BlockSpec((1, T, hb * 128), lambda b, g: (b, 0, g)),
                  pl.BlockSpec((1, hb, 4, S // C, C), lambda b, g: (b, g, 0, 0, 0)),
                  pl.BlockSpec((1, T, hb * 128), lambda b, g: (b, 0, g)),
                  pl.BlockSpec((1, 128), lambda b, g: (0, 0))] + [full(a) for a in consts],
        out_specs=pl.BlockSpec((1, T, hb * 128), lambda b, g: (b, 0, g)),
        scratch_shapes=[pltpu.VMEM((T, hb * 128), F32)],
        compiler_params=_cparams(2), name="deltanet",
    )(a_log, dt_bias, k, kT, v, q, ab, z, g_dn, *consts)


def _merge_body(a_ref, b_ref, wa_ref, wb_ref, ga_ref, gb_ref, o_ref):
    oa = jnp.dot(a_ref[0], wa_ref[...], preferred_element_type=F32)
    ob = jnp.dot(b_ref[0], wb_ref[...], preferred_element_type=F32)
    o_ref[0] = (ga_ref[0].astype(F32) * oa + gb_ref[0].astype(F32) * ob).astype(o_ref.dtype)


def _merge(att, ob, w_oa, w_ob, gates):
    B, T, Ka = att.shape
    D = w_oa.shape[1]
    tm = _pick_tile(T, 1024, 8)
    tn = _pick_tile(D, 512)
    nj = D // tn
    return pl.pallas_call(
        _merge_body,
        out_shape=jax.ShapeDtypeStruct((B, T, D), BF16),
        grid=(B, T // tm, nj),
        in_specs=[pl.BlockSpec((1, tm, Ka), lambda b, i, j: (b, i, 0)),
                  pl.BlockSpec((1, tm, Ka), lambda b, i, j: (b, i, 0)),
                  pl.BlockSpec((Ka, tn), lambda b, i, j: (0, j)),
                  pl.BlockSpec((Ka, tn), lambda b, i, j: (0, j)),
                  pl.BlockSpec((1, tm, tn), lambda b, i, j: (b, i, j)),
                  pl.BlockSpec((1, tm, tn), lambda b, i, j: (b, i, j + nj))],
        out_specs=pl.BlockSpec((1, tm, tn), lambda b, i, j: (b, i, j)),
        compiler_params=_cparams(3), name="merge",
    )(att, ob, w_oa, w_ob, gates, gates)


def _norm_router_body(x_ref, g_ref, sh_ref, sc_ref, wh_ref, wl_ref, br_ref, h_ref, id_ref, gate_ref):
    h = _rms(x_ref[0], g_ref[...]) * (1.0 + sc_ref[0]) + sh_ref[0]
    h_ref[0] = h.astype(h_ref.dtype)
    hh = h.astype(BF16)
    hl = (h - hh.astype(F32)).astype(BF16)
    logits = (jnp.dot(hh, wh_ref[...], preferred_element_type=F32)
              + jnp.dot(hl, wh_ref[...], preferred_element_type=F32)
              + jnp.dot(hh, wl_ref[...], preferred_element_type=F32)) + br_ref[...]
    lane = lax.broadcasted_iota(jnp.int32, logits.shape, 1)
    neg = jnp.float32(-jnp.inf)
    big = jnp.int32(1 << 20)
    is_g = lane < N_GROUPS
    lg = jnp.where(is_g, logits, neg)
    mg = jnp.max(lg, axis=-1, keepdims=True)
    eg = jnp.where(is_g, jnp.exp(lg - mg), 0.0)
    pg = eg / jnp.sum(eg, axis=-1, keepdims=True)
    pg_top = jnp.max(pg, axis=-1, keepdims=True)
    g_top = jnp.min(jnp.where(jnp.logical_and(is_g, pg == pg_top), lane, big), axis=-1, keepdims=True)
    lo = N_GROUPS + EXPERTS_PER_GROUP * g_top
    is_e = jnp.logical_and(lane >= lo, lane < lo + EXPERTS_PER_GROUP)
    le = jnp.where(is_e, logits, neg)
    me = jnp.max(le, axis=-1, keepdims=True)
    ee = jnp.where(is_e, jnp.exp(le - me), 0.0)
    pe = ee / jnp.sum(ee, axis=-1, keepdims=True)
    p1 = jnp.max(pe, axis=-1, keepdims=True)
    i1 = jnp.min(jnp.where(jnp.logical_and(is_e, pe == p1), lane, big), axis=-1, keepdims=True)
    rest = jnp.logical_and(is_e, lane != i1)
    pe2 = jnp.where(rest, pe, -1.0)
    p2 = jnp.max(pe2, axis=-1, keepdims=True)
    i2 = jnp.min(jnp.where(jnp.logical_and(rest, pe2 == p2), lane, big), axis=-1, keepdims=True)
    den = p1 + p2
    id_ref[0] = jnp.where(lane == 0, i1 - N_GROUPS, jnp.where(lane == 1, i2 - N_GROUPS, 0))
    gate_ref[0] = jnp.where(lane == 0, p1 / den * pg_top, jnp.where(lane == 1, p2 / den * pg_top, 0.0))


def _norm_router(x, g, shift, scale, w_hi, w_lo, b_r):
    B, T, D = x.shape
    tr = _pick_tile(T, 256, 8)
    return pl.pallas_call(
        _norm_router_body,
        out_shape=(jax.ShapeDtypeStruct((B, T, D), BF16),
                   jax.ShapeDtypeStruct((B, T, 128), jnp.int32),
                   jax.ShapeDtypeStruct((B, T, 128), F32)),
        grid=(B, T // tr),
        in_specs=[pl.BlockSpec((1, tr, D), lambda b, i: (b, i, 0)),
                  pl.BlockSpec((1, D), lambda b, i: (0, 0)),
                  pl.BlockSpec((1, 1, D), lambda b, i: (b, 0, 0)),
                  pl.BlockSpec((1, 1, D), lambda b, i: (b, 0, 0)),
                  pl.BlockSpec((D, 128), lambda b, i: (0, 0)),
                  pl.BlockSpec((D, 128), lambda b, i: (0, 0)),
                  pl.BlockSpec((1, 128), lambda b, i: (0, 0))],
        out_specs=(pl.BlockSpec((1, tr, D), lambda b, i: (b, i, 0)),
                   pl.BlockSpec((1, tr, 128), lambda b, i: (b, i, 0)),
                   pl.BlockSpec((1, tr, 128), lambda b, i: (b, i, 0))),
        compiler_params=_cparams(2), name="norm_router",
    )(x, g, shift, scale, w_hi, w_lo, b_r)


def _moe_up_body(be_ref, nu_ref, x_ref, w1_ref, w3_ref, o_ref, w1b_ref, w3b_ref):
    i = pl.program_id(1)
    changed = jnp.logical_or(i == 0, be_ref[i] != be_ref[jnp.maximum(i - 1, 0)])

    @pl.when(changed)
    def _():
        w1b_ref[...] = w1_ref[0].astype(BF16)
        w3b_ref[...] = w3_ref[0].astype(BF16)

    @pl.when(i < nu_ref[0])
    def _():
        x = x_ref[...]
        a = jnp.dot(x, w1b_ref[...], preferred_element_type=F32)
        b = jnp.dot(x, w3b_ref[...], preferred_element_type=F32)
        o_ref[...] = (_silu(a) * b).astype(o_ref.dtype)

    @pl.when(i >= nu_ref[0])
    def _():
        o_ref[...] = jnp.zeros(o_ref.shape, o_ref.dtype)


def _moe_down_body(be_ref, nu_ref, h_ref, w2_ref, o_ref, w2b_ref):
    i = pl.program_id(1)
    changed = jnp.logical_or(i == 0, be_ref[i] != be_ref[jnp.maximum(i - 1, 0)])

    @pl.when(changed)
    def _():
        w2b_ref[...] = w2_ref[0].astype(BF16)

    @pl.when(i < nu_ref[0])
    def _():
        o_ref[...] = jnp.dot(h_ref[...], w2b_ref[...], preferred_element_type=F32).astype(o_ref.dtype)

    @pl.when(i >= nu_ref[0])
    def _():
        o_ref[...] = jnp.zeros(o_ref.shape, o_ref.dtype)


def _moe_experts(x_pad, blk_expert, n_used, w1, w3, w2):
    R, D = x_pad.shape
    nb = R // MOE_ROWS
    Hd = w1.shape[2]
    th = _pick_tile(Hd, 512)
    hbuf = pl.pallas_call(
        _moe_up_body,
        out_shape=jax.ShapeDtypeStruct((R, Hd), BF16),
        grid_spec=pltpu.PrefetchScalarGridSpec(
            num_scalar_prefetch=2, grid=(Hd // th, nb),
            in_specs=[pl.BlockSpec((MOE_ROWS, D), lambda j, i, be, nu: (i, 0)),
                      pl.BlockSpec((1, D, th), lambda j, i, be, nu: (be[i], 0, j)),
                      pl.BlockSpec((1, D, th), lambda j, i, be, nu: (be[i], 0, j))],
            out_specs=pl.BlockSpec((MOE_ROWS, th), lambda j, i, be, nu: (i, j)),
            scratch_shapes=[pltpu.VMEM((D, th), BF16), pltpu.VMEM((D, th), BF16)]),
        compiler_params=_cparams(2), name="moe_up",
    )(blk_expert, n_used, x_pad, w1, w3)
    tn = _pick_tile(D, 1024)
    return pl.pallas_call(
        _moe_down_body,
        out_shape=jax.ShapeDtypeStruct((R, D), BF16),
        grid_spec=pltpu.PrefetchScalarGridSpec(
            num_scalar_prefetch=2, grid=(D // tn, nb),
            in_specs=[pl.BlockSpec((MOE_ROWS, Hd), lambda j, i, be, nu: (i, 0)),
                      pl.BlockSpec((1, Hd, tn), lambda j, i, be, nu: (be[i], 0, j))],
            out_specs=pl.BlockSpec((MOE_ROWS, tn), lambda j, i, be, nu: (i, j)),
            scratch_shapes=[pltpu.VMEM((Hd, tn), BF16)]),
        compiler_params=_cparams(2), name="moe_down",
    )(blk_expert, n_used, hbuf, w2)


def _combine_body(x_ref, y0_ref, y1_ref, p_ref, g_ref, o_ref):
    p = p_ref[0]
    ffn = p[:, 0:1] * y0_ref[0].astype(F32) + p[:, 1:2] * y1_ref[0].astype(F32)
    o_ref[0] = x_ref[0] + g_ref[0] * ffn


def _combine(x, y0, y1, gate, g2):
    B, T, D = x.shape
    tr = _pick_tile(T, 256, 8)
    row = pl.BlockSpec((1, tr, D), lambda b, i: (b, i, 0))
    return pl.pallas_call(
        _combine_body,
        out_shape=jax.ShapeDtypeStruct((B, T, D), F32),
        grid=(B, T // tr),
        in_specs=[row, row, row,
                  pl.BlockSpec((1, tr, 128), lambda b, i: (b, i, 0)),
                  pl.BlockSpec((1, 1, D), lambda b, i: (b, 0, 0))],
        out_specs=row,
        compiler_params=_cparams(2), name="moe_combine",
    )(x, y0, y1, gate, g2)


def _rope_tables(T, Tc):
    rows = T // GRID_W
    row = jnp.repeat(jnp.arange(rows), GRID_W).astype(F32)
    col = jnp.tile(jnp.arange(GRID_W), rows).astype(F32)
    half = QK_ROPE // 2
    inv_freq = ROPE_THETA ** (-jnp.arange(0, half, 2, dtype=F32) / half)
    ang = jnp.concatenate([row[:, None] * inv_freq, col[:, None] * inv_freq], axis=-1)
    cos, sin = jnp.cos(ang), jnp.sin(ang)
    z64 = jnp.zeros((T, 64), F32)
    lat = jnp.concatenate([cos, cos, z64, -sin, sin, z64], axis=-1)
    ctx = jnp.concatenate([jnp.ones((Tc, 64), F32), jnp.zeros((Tc, 192), F32)], axis=-1)
    return jnp.concatenate([lat, ctx], axis=0)


_PERM = np.concatenate([np.arange(0, QK_ROPE, 2), np.arange(1, QK_ROPE, 2)])
_PERM_SW = np.concatenate([np.arange(1, QK_ROPE, 2), np.arange(0, QK_ROPE, 2)])


def _pad_cols(a, n):
    return jnp.pad(a, ((0, 0), (0, n - a.shape[1])))


def _moe_dispatch(expert_ids, n_tokens):
    a = n_tokens * TOP_K
    e_flat = expert_ids.reshape(a)
    onehot = (e_flat[:, None] == jnp.arange(N_EXPERTS)[None, :]).astype(jnp.int32)
    csum = jnp.cumsum(onehot, axis=0)
    rank = jnp.take_along_axis(csum, e_flat[:, None], axis=1)[:, 0] - 1
    counts = csum[-1]
    padded = (counts + MOE_ROWS - 1) // MOE_ROWS * MOE_ROWS
    pad_end = jnp.cumsum(padded)
    pad_start = pad_end - padded
    dest = pad_start[e_flat] + rank
    n_blocks = -(-a // MOE_ROWS) + N_EXPERTS
    src = jnp.zeros((n_blocks * MOE_ROWS,), jnp.int32).at[dest].set(jnp.arange(a, dtype=jnp.int32) // TOP_K)
    blk_expert = jnp.minimum(
        jnp.searchsorted(pad_end, jnp.arange(n_blocks) * MOE_ROWS, side="right"), N_EXPERTS - 1).astype(jnp.int32)
    n_used = (pad_end[-1] // MOE_ROWS).astype(jnp.int32).reshape(1)
    return dest, src, blk_expert, n_used


def _layer(x, c, ctx, c_ctx, p):
    B, T, D = x.shape
    Tc = ctx.shape[1]
    S = T + Tc
    H = MLA_HEADS

    cin = jnp.zeros((1, 16, D), F32).at[0, :B].set(c).at[0, B].set(c_ctx)
    mod = _matmul(cin, p["w_mod"], tm=16, tn=1024, out_dtype=F32, pre=_silu,
                  epi=lambda acc, b: acc + b, extras=[(p["b_mod"].reshape(1, 1, -1), "bcol")], name="mod")[0]
    mod_l = mod[:B].reshape(B, N_MOD, 1, D)
    mod_c = mod[B].reshape(N_MOD, 1, 1, D)

    h = _norm_mod(x, ctx, p["norm1_g"].reshape(1, D), mod_l[:, 0], mod_l[:, 1], mod_c[0], mod_c[1])

    w_in = p["w_in"]
    wkr = w_in[:, OFF_KR:OFF_AB]
    w_small = jnp.concatenate([w_in[:, OFF_CKV:OFF_KR], _pad_cols(wkr[:, _PERM], 128),
                               _pad_cols(wkr[:, _PERM_SW], 128), _pad_cols(w_in[:, OFF_AB:OFF_DK], 128)],
                              axis=1).astype(BF16)
    p_small = _matmul(h, w_small, tm=1152, tn=896, out_dtype=F32, name="in_small")
    p_dkv = _matmul(h, w_in[:, OFF_DK:OFF_DQ].astype(BF16), tm=1152, name="in_dkv")
    p_dq = _matmul(h, w_in[:, OFF_DQ:OFF_CQ].astype(BF16), rows=T, name="in_dq")
    p_cq = _matmul(h, w_in[:, OFF_CQ:OFF_Z].astype(BF16), rows=T, out_dtype=F32, name="in_cq")
    z_act = _matmul(h, w_in[:, OFF_Z:OFF_GATE].astype(BF16), rows=T, epi=_silu, name="in_z")
    gates = _matmul(h, w_in[:, OFF_GATE:].astype(BF16), rows=T, epi=jax.nn.sigmoid, name="in_gate")

    tab = _rope_tables(T, Tc)
    gkr = p["k_rope_norm_g"]
    ckv_n, k_rope = _mla_small(p_small, tab, p["kv_norm_g"].reshape(1, -1),
                               _pad_cols(gkr[_PERM][None], 128), _pad_cols(gkr[_PERM_SW][None], 128))
    w_ukv = p["w_ukv"].reshape(KV_LORA, H, QK_NOPE + V_HEAD)
    hg = HEAD_GROUP
    w_kv_perm = jnp.concatenate(
        [w_ukv[:, :, :QK_NOPE].reshape(KV_LORA, H // hg, hg * QK_NOPE),
         w_ukv[:, :, QK_NOPE:].reshape(KV_LORA, H // hg, hg * V_HEAD)], axis=2).reshape(KV_LORA, H * 256).astype(BF16)
    k_all, v_all = _kv_up(ckv_n, w_kv_perm, k_rope, p["k_norm_g"].reshape(1, -1))

    w_uq = p["w_uq"].reshape(Q_LORA, H, QK_NOPE + QK_ROPE)
    wq_r = w_uq[:, :, QK_NOPE:]
    zpad = jnp.zeros((Q_LORA, H, 64), F32)
    w_q_perm = jnp.concatenate(
        [w_uq[:, :, :QK_NOPE].reshape(Q_LORA, H // hg, hg * 128),
         jnp.concatenate([wq_r[:, :, _PERM], zpad], axis=2).reshape(Q_LORA, H // hg, hg * 128),
         jnp.concatenate([wq_r[:, :, _PERM_SW], zpad], axis=2).reshape(Q_LORA, H // hg, hg * 128)],
        axis=2).reshape(Q_LORA, H * 384).astype(BF16)
    gqr = p["q_rope_norm_g"]
    q_all = _q_up(p_cq, w_q_perm, tab[:T], p["q_norm_g"].reshape(1, -1), p["q_a_norm_g"].reshape(1, -1),
                  _pad_cols(gqr[_PERM][None], 128), _pad_cols(gqr[_PERM_SW][None], 128))
    att = _attention(q_all, k_all, v_all)

    cw = jnp.pad(p["conv_w"], ((0, 8 - CONV_W), (0, 0)))
    dn_k = _short_conv(p_dkv, 0, DN_QK, cw[:, :DN_QK], n_lat_rows=T, l2=True)
    dn_v = _short_conv(p_dkv, DN_QK, DN_V, cw[:, DN_QK:DN_QK + DN_V], n_lat_rows=T, l2=False)
    dn_q = _short_conv(p_dq, 0, DN_QK, cw[:, DN_QK + DN_V:], n_lat_rows=T, l2=True, scale=float(DN_DK ** -0.5))
    C = DN_CHUNK
    dn_kT = dn_k.reshape(B, S // C, C, DN_HEADS, DN_DK).transpose(0, 3, 1, 4, 2)
    ab = p_small[:, :, 768:768 + 4 * DN_HEADS].reshape(B, S // C, C, 4, DN_HEADS).transpose(0, 4, 3, 1, 2)
    ob_in = _deltanet(dn_k, dn_kT, dn_v, dn_q, ab, z_act, p["a_log"], p["dt_bias"], p["dn_norm_g"].reshape(1, -1))

    mix = _merge(att, ob_in, p["w_oa"].astype(BF16), p["w_ob"].astype(BF16), gates)
    xl = _matmul(mix, p["w_out"].astype(BF16), out_dtype=F32,
                 epi=lambda acc, xr, g: xr + g * acc, extras=[(x, "tile"), (mod_l[:, 2], "bcol")], name="out_proj")

    w_r = _pad_cols(jnp.concatenate([p["w_rg"], p["w_re"]], axis=1), 128)
    w_r_hi = w_r.astype(BF16)
    w_r_lo = (w_r - w_r_hi.astype(F32)).astype(BF16)
    b_r = _pad_cols(jnp.concatenate([p["b_rg"], p["b_re"]])[None], 128)
    h2, ids, gate = _norm_router(xl, p["norm2_g"].reshape(1, D), mod_l[:, 3], mod_l[:, 4], w_r_hi, w_r_lo, b_r)
    n = B * T
    dest, src, blk_expert, n_used = _moe_dispatch(ids[:, :, :TOP_K].reshape(n, TOP_K), n)
    x_pad = jnp.take(h2.reshape(n, D), src, axis=0)
    y_pad = _moe_experts(x_pad, blk_expert, n_used, p["w1"], p["w3"], p["w2"])
    dest = dest.reshape(n, TOP_K)
    y0 = jnp.take(y_pad, dest[:, 0], axis=0).reshape(B, T, D)
    y1 = jnp.take(y_pad, dest[:, 1], axis=0).reshape(B, T, D)
    return _combine(xl, y0, y1, gate, mod_l[:, 5])


def kernel(x, c, ctx, c_ctx, norm1_g, norm2_g, w_mod, b_mod, w_in, q_a_norm_g, w_uq, kv_norm_g, w_ukv, q_norm_g,
           q_rope_norm_g, k_norm_g, k_rope_norm_g, conv_w, a_log, dt_bias, dn_norm_g, w_oa, w_ob, w_out, w_rg, b_rg,
           w_re, b_re, w1, w3, w2):
    depth = norm1_g.shape[0]
    assert depth == 1, "single-layer block: the context stream is read, never updated"
    layer = 0
    p = {
        "norm1_g": norm1_g[layer], "norm2_g": norm2_g[layer], "w_mod": w_mod[layer], "b_mod": b_mod[layer],
        "w_in": w_in[layer], "q_a_norm_g": q_a_norm_g[layer], "w_uq": w_uq[layer], "kv_norm_g": kv_norm_g[layer],
        "w_ukv": w_ukv[layer], "q_norm_g": q_norm_g[layer], "q_rope_norm_g": q_rope_norm_g[layer],
        "k_norm_g": k_norm_g[layer], "k_rope_norm_g": k_rope_norm_g[layer], "conv_w": conv_w[layer],
        "a_log": a_log[layer], "dt_bias": dt_bias[layer], "dn_norm_g": dn_norm_g[layer], "w_oa": w_oa[layer],
        "w_ob": w_ob[layer], "w_out": w_out[layer], "w_rg": w_rg[layer], "b_rg": b_rg[layer], "w_re": w_re[layer],
        "b_re": b_re[layer], "w1": w1[layer], "w3": w3[layer], "w2": w2[layer],
    }
    return _layer(x, c, ctx, c_ctx, p)
```

```python
import functools

import numpy as np
import jax
import jax.numpy as jnp
from jax import lax
from jax.experimental import pallas as pl
from jax.experimental.pallas import tpu as pltpu

F32 = jnp.float32
BF16 = jnp.bfloat16

EPS = 1e-6
GRID_W = 64
N_MOD = 6
MLA_HEADS = 16
Q_LORA = 1024
KV_LORA = 512
QK_NOPE = 128
QK_ROPE = 64
V_HEAD = 128
ROPE_THETA = 10000.0
DN_HEADS = 16
DN_DK = 128
DN_DV = 128
DN_QK = DN_HEADS * DN_DK
DN_V = DN_HEADS * DN_DV
CONV_W = 5
N_GROUPS = 4
EXPERTS_PER_GROUP = 8
N_EXPERTS = N_GROUPS * EXPERTS_PER_GROUP
TOP_K = 2
EXPERT_HIDDEN = 1024

OFF_CKV = 0
OFF_KR = OFF_CKV + KV_LORA
OFF_AB = OFF_KR + QK_ROPE
OFF_DK = OFF_AB + 4 * DN_HEADS
OFF_DV = OFF_DK + DN_QK
OFF_DQ = OFF_DV + DN_V
OFF_CQ = OFF_DQ + DN_QK
OFF_Z = OFF_CQ + Q_LORA
OFF_GATE = OFF_Z + DN_V

V7X_LANES = 128
V7X_VMEM_BYTES = 64 * 1024 * 1024
VMEM_LIMIT = V7X_VMEM_BYTES - 8 * 1024 * 1024

DN_CHUNK = 128
MOE_ROWS = 256
HEAD_GROUP = 4
DN_HEADS_PER_STEP = 4
ATTN_SPLIT = 2
CONV_HALO = 16


def _cparams(n_grid):
    return pltpu.CompilerParams(dimension_semantics=("arbitrary",) * n_grid, vmem_limit_bytes=VMEM_LIMIT)


def _pick_tile(n, target, mult=V7X_LANES):
    if n <= target:
        return n
    best = None
    t = mult
    while t <= target:
        if n % t == 0:
            best = t
        t += mult
    assert best is not None, (n, target)
    return best


def _rms(x, g):
    return x * lax.rsqrt(jnp.mean(x * x, axis=-1, keepdims=True) + EPS) * g


def _silu(x):
    return x * jax.nn.sigmoid(x)


def _bdot(a, b):
    return jnp.dot(a, b, preferred_element_type=F32)


def _mm_body(*refs, pre, epi, n_extra, cached):
    x_ref, w_ref = refs[0], refs[1]
    extra = refs[2:2 + n_extra]
    o_ref = refs[2 + n_extra]
    if cached:
        xs_ref = refs[3 + n_extra]

        @pl.when(pl.program_id(2) == 0)
        def _():
            xs_ref[...] = pre(x_ref[0]).astype(BF16)

        x = xs_ref[...]
    else:
        x = x_ref[0]
    w = w_ref[...]
    if w.dtype != BF16:
        w = w.astype(BF16)
    acc = _bdot(x, w)
    if epi is not None:
        acc = epi(acc, *[e[0] for e in extra])
    o_ref[0] = acc.astype(o_ref.dtype)


def _matmul(x, w, *, rows=None, tm=1024, tn=512, out_dtype=BF16, pre=None, epi=None, extras=(), name="mm"):
    B, S, K = x.shape
    N = w.shape[1]
    rows = S if rows is None else rows
    tm = _pick_tile(rows, tm)
    tn = _pick_tile(N, tn)
    grid = (B, rows // tm, N // tn)
    in_specs = [pl.BlockSpec((1, tm, K), lambda b, i, j: (b, i, 0)),
                pl.BlockSpec((K, tn), lambda b, i, j: (0, j))]
    args = [x, w]
    for arr, kind in extras:
        if kind == "tile":
            in_specs.append(pl.BlockSpec((1, tm, tn), lambda b, i, j: (b, i, j)))
        elif kind == "bcol":
            if arr.shape[0] == 1:
                in_specs.append(pl.BlockSpec((1, 1, tn), lambda b, i, j: (0, 0, j)))
            else:
                in_specs.append(pl.BlockSpec((1, 1, tn), lambda b, i, j: (b, 0, j)))
        else:
            raise ValueError(kind)
        args.append(arr)
    cached = pre is not None
    scratch = [pltpu.VMEM((tm, K), BF16)] if cached else []
    if not cached:
        assert x.dtype == BF16
    return pl.pallas_call(
        functools.partial(_mm_body, pre=pre, epi=epi, n_extra=len(extras), cached=cached),
        out_shape=jax.ShapeDtypeStruct((B, rows, N), out_dtype),
        grid=grid, in_specs=in_specs,
        out_specs=pl.BlockSpec((1, tm, tn), lambda b, i, j: (b, i, j)),
        scratch_shapes=scratch, compiler_params=_cparams(3), name=name,
    )(*args)


def _norm_mod_body(x_ref, c_ref, g_ref, sl_ref, scl_ref, sc_ref, scc_ref, o_ref, *, n_lat):
    j = pl.program_id(1)

    @pl.when(j < n_lat)
    def _():
        h = _rms(x_ref[0], g_ref[...])
        o_ref[0] = (h * (1.0 + scl_ref[0]) + sl_ref[0]).astype(o_ref.dtype)

    @pl.when(j >= n_lat)
    def _():
        h = _rms(c_ref[0], g_ref[...])
        o_ref[0] = (h * (1.0 + scc_ref[0]) + sc_ref[0]).astype(o_ref.dtype)


def _norm_mod(x, ctx, g, shift_l, scale_l, shift_c, scale_c):
    B, T, D = x.shape
    Tc = ctx.shape[1]
    tr = _pick_tile(int(np.gcd(T, Tc)), 256, 8)
    n_lat, n_ctx = T // tr, Tc // tr
    return pl.pallas_call(
        functools.partial(_norm_mod_body, n_lat=n_lat),
        out_shape=jax.ShapeDtypeStruct((B, T + Tc, D), BF16),
        grid=(B, n_lat + n_ctx),
        in_specs=[pl.BlockSpec((1, tr, D), lambda b, j: (b, jnp.minimum(j, n_lat - 1), 0)),
                  pl.BlockSpec((1, tr, D), lambda b, j: (b, jnp.maximum(j - n_lat, 0), 0)),
                  pl.BlockSpec((1, D), lambda b, j: (0, 0)),
                  pl.BlockSpec((1, 1, D), lambda b, j: (b, 0, 0)),
                  pl.BlockSpec((1, 1, D), lambda b, j: (b, 0, 0)),
                  pl.BlockSpec((1, 1, D), lambda b, j: (0, 0, 0)),
                  pl.BlockSpec((1, 1, D), lambda b, j: (0, 0, 0))],
        out_specs=pl.BlockSpec((1, tr, D), lambda b, j: (b, j, 0)),
        compiler_params=_cparams(2), name="norm_mod",
    )(x, ctx, g, shift_l, scale_l, shift_c, scale_c)


def _mla_small_body(p_ref, tab_ref, gkv_ref, gkr_ref, gks_ref, ckv_ref, kr_ref):
    p = p_ref[0]
    ckv_ref[0] = _rms(p[:, :KV_LORA], gkv_ref[...]).astype(ckv_ref.dtype)
    xr = p[:, KV_LORA:KV_LORA + 128]
    xs = p[:, KV_LORA + 128:KV_LORA + 256]
    inv = lax.rsqrt(jnp.sum(xr * xr, axis=-1, keepdims=True) * (1.0 / QK_ROPE) + EPS)
    tab = tab_ref[...]
    kr = inv * (xr * gkr_ref[...] * tab[:, :128] + xs * gks_ref[...] * tab[:, 128:])
    kr_ref[0] = kr.astype(kr_ref.dtype)


def _mla_small(p_small, tab, g_kv, g_kr, g_ks):
    B, S, W = p_small.shape
    tr = _pick_tile(S, 768, 8)
    return pl.pallas_call(
        _mla_small_body,
        out_shape=(jax.ShapeDtypeStruct((B, S, KV_LORA), BF16), jax.ShapeDtypeStruct((B, S, 128), BF16)),
        grid=(B, S // tr),
        in_specs=[pl.BlockSpec((1, tr, 768), lambda b, i: (b, i, 0)),
                  pl.BlockSpec((tr, 256), lambda b, i: (i, 0)),
                  pl.BlockSpec((1, KV_LORA), lambda b, i: (0, 0)),
                  pl.BlockSpec((1, 128), lambda b, i: (0, 0)),
                  pl.BlockSpec((1, 128), lambda b, i: (0, 0))],
        out_specs=(pl.BlockSpec((1, tr, KV_LORA), lambda b, i: (b, i, 0)),
                   pl.BlockSpec((1, tr, 128), lambda b, i: (b, i, 0))),
        compiler_params=_cparams(2), name="mla_small",
    )(p_small, tab, g_kv, g_kr, g_ks)


def _kv_up_body(x_ref, w_ref, kr_ref, gk_ref, k_ref, v_ref, *, hg):
    acc = _bdot(x_ref[0], w_ref[...])
    kr = kr_ref[0]
    for h in range(hg):
        kn = _rms(acc[:, h * 128:(h + 1) * 128], gk_ref[...])
        k_ref[0, :, h * 256:h * 256 + 128] = kn.astype(k_ref.dtype)
        k_ref[0, :, h * 256 + 128:(h + 1) * 256] = kr
    v_ref[0] = acc[:, hg * 128:].astype(v_ref.dtype)


def _kv_up(ckv, w_perm, kr, g_k):
    B, S, _ = ckv.shape
    hg = HEAD_GROUP
    tm = _pick_tile(S, 1152, 8)
    return pl.pallas_call(
        functools.partial(_kv_up_body, hg=hg),
        out_shape=(jax.ShapeDtypeStruct((B, S, MLA_HEADS * 256), BF16),
                   jax.ShapeDtypeStruct((B, S, MLA_HEADS * V_HEAD), BF16)),
        grid=(B, S // tm, MLA_HEADS // hg),
        in_specs=[pl.BlockSpec((1, tm, KV_LORA), lambda b, i, j: (b, i, 0)),
                  pl.BlockSpec((KV_LORA, hg * 256), lambda b, i, j: (0, j)),
                  pl.BlockSpec((1, tm, 128), lambda b, i, j: (b, i, 0)),
                  pl.BlockSpec((1, 128), lambda b, i, j: (0, 0))],
        out_specs=(pl.BlockSpec((1, tm, hg * 256), lambda b, i, j: (b, i, j)),
                   pl.BlockSpec((1, tm, hg * 128), lambda b, i, j: (b, i, j))),
        compiler_params=_cparams(3), name="kv_up",
    )(ckv, w_perm, kr, g_k)


def _q_up_body(x_ref, w_ref, tab_ref, gq_ref, gqa_ref, gr_ref, gs_ref, q_ref, xs_ref, *, hg, scale):
    @pl.when(pl.program_id(2) == 0)
    def _():
        xs_ref[...] = _rms(x_ref[0], gqa_ref[...]).astype(BF16)

    acc = _bdot(xs_ref[...], w_ref[...])
    tab = tab_ref[...]
    for h in range(hg):
        qn = _rms(acc[:, h * 128:(h + 1) * 128], gq_ref[...]) * scale
        xr = acc[:, (hg + h) * 128:(hg + h + 1) * 128]
        xw = acc[:, (2 * hg + h) * 128:(2 * hg + h + 1) * 128]
        inv = lax.rsqrt(jnp.sum(xr * xr, axis=-1, keepdims=True) * (1.0 / QK_ROPE) + EPS) * scale
        qr = inv * (xr * gr_ref[...] * tab[:, :128] + xw * gs_ref[...] * tab[:, 128:])
        q_ref[0, :, h * 256:h * 256 + 128] = qn.astype(q_ref.dtype)
        q_ref[0, :, h * 256 + 128:(h + 1) * 256] = qr.astype(q_ref.dtype)


def _q_up(cq, w_perm, tab, g_q, g_qa, g_r, g_s):
    B, T, _ = cq.shape
    hg = HEAD_GROUP
    tm = _pick_tile(T, 512, 8)
    scale = float((QK_NOPE + QK_ROPE) ** -0.5 * np.log2(np.e))
    return pl.pallas_call(
        functools.partial(_q_up_body, hg=hg, scale=scale),
        out_shape=jax.ShapeDtypeStruct((B, T, MLA_HEADS * 256), BF16),
        grid=(B, T // tm, MLA_HEADS // hg),
        in_specs=[pl.BlockSpec((1, tm, Q_LORA), lambda b, i, j: (b, i, 0)),
                  pl.BlockSpec((Q_LORA, hg * 384), lambda b, i, j: (0, j)),
                  pl.BlockSpec((tm, 256), lambda b, i, j: (i, 0)),
                  pl.BlockSpec((1, 128), lambda b, i, j: (0, 0)),
                  pl.BlockSpec((1, Q_LORA), lambda b, i, j: (0, 0)),
                  pl.BlockSpec((1, 128), lambda b, i, j: (0, 0)),
                  pl.BlockSpec((1, 128), lambda b, i, j: (0, 0))],
        out_specs=pl.BlockSpec((1, tm, hg * 256), lambda b, i, j: (b, i, j)),
        scratch_shapes=[pltpu.VMEM((tm, Q_LORA), BF16)],
        compiler_params=_cparams(3), name="q_up",
    )(cq, w_perm, tab, g_q, g_qa, g_r, g_s)


def _attn_body(q_ref, k_ref, v_ref, o_ref, *, n_split):
    k, v = k_ref[0], v_ref[0]
    rows = q_ref.shape[1] // n_split
    sl = [slice(i * rows, (i + 1) * rows) for i in range(n_split)]
    s = [lax.dot_general(q_ref[0, r, :], k, (((1,), (1,)), ((), ())), preferred_element_type=F32) for r in sl]
    m = [jnp.max(x, axis=-1, keepdims=True) for x in s]
    p = [jnp.exp2(x - mx) for x, mx in zip(s, m)]
    l = [jnp.sum(x, axis=-1, keepdims=True) for x in p]
    o = [_bdot(x.astype(BF16), v) for x in p]
    for r, ox, lx in zip(sl, o, l):
        o_ref[0, r, :] = (ox / lx).astype(o_ref.dtype)


def _attention(q, k, v):
    B, T, _ = q.shape
    S = k.shape[1]
    tq = _pick_tile(T, 512, 8)
    return pl.pallas_call(
        functools.partial(_attn_body, n_split=ATTN_SPLIT if tq % (8 * ATTN_SPLIT) == 0 else 1),
        out_shape=jax.ShapeDtypeStruct((B, T, MLA_HEADS * V_HEAD), BF16),
        grid=(B, MLA_HEADS, T // tq),
        in_specs=[pl.BlockSpec((1, tq, 256), lambda b, h, i: (b, i, h)),
                  pl.BlockSpec((1, S, 256), lambda b, h, i: (b, 0, h)),
                  pl.BlockSpec((1, S, V_HEAD), lambda b, h, i: (b, 0, h))],
        out_specs=pl.BlockSpec((1, tq, V_HEAD), lambda b, h, i: (b, i, h)),
        compiler_params=_cparams(3), name="mla_attention",
    )(q, k, v)


def _conv_body(prev_ref, cur_ref, next_ref, w_ref, o_ref, buf_ref, *, n_lat, n_tiles, l2, scale, tr):
    j = pl.program_id(1)
    tc = cur_ref.shape[2]
    prev_ok = jnp.logical_and(j != 0, j != n_lat)
    next_ok = jnp.logical_and(j != n_lat - 1, j != n_tiles - 1)
    zeros8 = jnp.zeros((8, tc), F32)
    buf_ref[0:8, :] = jnp.where(prev_ok, prev_ref[0].astype(F32)[CONV_HALO - 8:], zeros8)
    buf_ref[8:8 + tr, :] = cur_ref[0].astype(F32)
    buf_ref[8 + tr:16 + tr, :] = jnp.where(next_ok, next_ref[0].astype(F32)[:8], zeros8)
    w = w_ref[...]
    y = jnp.zeros((tr, tc), F32)
    for t in range(CONV_W):
        r0 = 8 + t - CONV_W // 2
        y = y + buf_ref[r0:r0 + tr, :] * w[t:t + 1, :]
    y = _silu(y)
    if l2:
        for h in range(tc // 128):
            yh = y[:, h * 128:(h + 1) * 128]
            yh = yh * lax.rsqrt(jnp.sum(yh * yh, axis=-1, keepdims=True) + EPS) * scale
            o_ref[0, :, h * 128:(h + 1) * 128] = yh.astype(o_ref.dtype)
    else:
        o_ref[0] = y.astype(o_ref.dtype)


def _short_conv(x, col0, width, w8, *, n_lat_rows, l2, scale=1.0):
    B, R, _ = x.shape
    tr = _pick_tile(int(np.gcd(n_lat_rows, R - n_lat_rows)) if R > n_lat_rows else n_lat_rows, 256, CONV_HALO)
    tc = _pick_tile(width, 1024)
    n_tiles, n_lat = R // tr, n_lat_rows // tr
    cb = col0 // tc
    hpt = tr // CONV_HALO
    n_halo = R // CONV_HALO
    assert col0 % tc == 0
    return pl.pallas_call(
        functools.partial(_conv_body, n_lat=n_lat, n_tiles=n_tiles, l2=l2, scale=scale, tr=tr),
        out_shape=jax.ShapeDtypeStruct((B, R, width), BF16),
        grid=(B, n_tiles, width // tc),
        in_specs=[pl.BlockSpec((1, CONV_HALO, tc), lambda b, j, c: (b, jnp.maximum(j * hpt - 1, 0), cb + c)),
                  pl.BlockSpec((1, tr, tc), lambda b, j, c: (b, j, cb + c)),
                  pl.BlockSpec((1, CONV_HALO, tc),
                               lambda b, j, c: (b, jnp.minimum((j + 1) * hpt, n_halo - 1), cb + c)),
                  pl.BlockSpec((8, tc), lambda b, j, c: (0, c))],
        out_specs=pl.BlockSpec((1, tr, tc), lambda b, j, c: (b, j, c)),
        scratch_shapes=[pltpu.VMEM((tr + 16, tc), F32)],
        compiler_params=_cparams(3), name="short_conv",
    )(x, x, x, w8)


def _dn_consts():
    C = DN_CHUNK
    r = np.arange(C)[:, None]
    c = np.arange(C)[None, :]
    tri = np.stack([(c <= r), (c >= r)]).astype(np.float32)
    ones = np.ones((C, C), np.float32)
    rhs = np.stack([np.concatenate([(r > c).astype(np.float32), ones], 1),
                    np.concatenate([(r < c).astype(np.float32), ones], 1)])
    rhs3 = np.concatenate([rhs, rhs, rhs], axis=1)
    incl = np.stack([(r >= c), (r <= c)]).astype(np.float32)
    strict = np.stack([(r > c), (r < c)]).astype(np.float32)
    levels = []
    s = 1
    while s < C:
        levels.append(((r // (2 * s) == c // (2 * s)) & (r // s != c // s)).astype(np.float32))
        s *= 2
    return (jnp.asarray(tri, BF16), jnp.asarray(rhs3, BF16), jnp.asarray(incl), jnp.asarray(strict),
            jnp.asarray(np.stack(levels)))


def _dn_chunks(ch, consts, with_q):
    C = DN_CHUNK
    tri_ref, rhs3_ref, incl_ref, strict_ref, lev_ref = consts
    n = range(len(ch))
    d = [c["d"] for c in ch]
    last = [C - 1 if di == 0 else 0 for di in d]
    la = [-c["a_scale"] * jax.nn.softplus(c["a_row"] + c["dt_b"]) for c in ch]
    beta = [jax.nn.sigmoid(c["b_row"]) for c in ch]
    dg = []
    for i in n:
        hi = la[i].astype(BF16)
        r1 = la[i] - hi.astype(F32)
        mid = r1.astype(BF16)
        lo = (r1 - mid.astype(F32)).astype(BF16)
        t = tri_ref[d[i]]
        dg.append(_bdot(jnp.concatenate([t * hi, t * mid, t * lo], axis=1), rhs3_ref[d[i]]))
    diff = [g[:, :C] for g in dg]
    gcol = [g[:, C:] for g in dg]
    tail_row = [jnp.exp(diff[i][last[i]:last[i] + 1, :]) for i in n]
    g_last = [jnp.exp(gcol[i][last[i]:last[i] + 1, :]) for i in n]
    dec = [incl_ref[d[i]] * jnp.exp(diff[i] * incl_ref[d[i]]) for i in n]
    egc = [jnp.exp(g) for g in gcol]
    gram = [_bdot(c["k"], c["kT"]) for c in ch]
    nd = [strict_ref[d[i]] * gram[i] * dec[i] * beta[i] for i in n]
    eye = incl_ref[0] * incl_ref[1]
    x = [eye - nd[i] * lev_ref[0] for i in n]
    for lv in range(1, lev_ref.shape[0]):
        xb = [xi.astype(BF16) for xi in x]
        m = [(nd[i] * lev_ref[lv]).astype(BF16) for i in n]
        xm = [_bdot(xb[i], m[i]).astype(BF16) for i in n]
        x = [x[i] - _bdot(xm[i], xb[i]) for i in n]
    ke = [(ch[i]["k"].astype(F32) * egc[i]).astype(BF16) for i in n]
    y = [_bdot(x[i].astype(BF16), jnp.concatenate([ch[i]["v"], ke[i]], axis=1)) for i in n]
    s_b = [c["S"].astype(BF16) for c in ch]
    if with_q:
        qg = [(ch[i]["q"].astype(F32) * egc[i]).astype(BF16) for i in n]
        r2 = [_bdot(jnp.concatenate([y[i][:, DN_DV:].astype(BF16), qg[i]], axis=0), s_b[i]) for i in n]
        yks = [r[:C] for r in r2]
        qs = [r[C:] for r in r2]
    else:
        yks = [_bdot(y[i][:, DN_DV:].astype(BF16), s_b[i]) for i in n]
    vt = [(y[i][:, :DN_DV] - yks[i]).astype(BF16) for i in n]
    kt_tail = [(ch[i]["kT"].astype(F32) * (beta[i] * tail_row[i])).astype(BF16) for i in n]
    s_new = [ch[i]["S"] * g_last[i] + _bdot(kt_tail[i], vt[i]) for i in n]
    if not with_q:
        return s_new, [None for _ in n]
    qk = [_bdot(c["q"], c["kT"]) for c in ch]
    p = [(qk[i] * dec[i] * beta[i]).astype(BF16) for i in n]
    o = [qs[i] + _bdot(p[i], vt[i]) for i in n]
    return s_new, o


def _dn_body(alog_ref, dtb_ref, k_ref, kT_ref, v_ref, q_ref, ab_ref, z_ref, g_ref,
             tri_ref, rhs3_ref, incl_ref, strict_ref, lev_ref, o_ref, acc_ref, s_ref, *, hb, n_lat, n_ctx):
    C = DN_CHUNK
    hg = pl.program_id(1)
    consts = (tri_ref, rhs3_ref, incl_ref, strict_ref, lev_ref)
    acc_ref[...] = jnp.zeros(acc_ref.shape, F32)
    s_ref[...] = jnp.zeros(s_ref.shape, F32)

    def run_chunk(cf, cb, with_q):
        ch, where = [], []
        for h in range(hb):
            head = hg * hb + h
            for d, c in ((0, cf), (1, cb)):
                row0 = pl.multiple_of(c * C, C)
                cols = slice(h * 128, (h + 1) * 128)
                ch.append(dict(
                    k=k_ref[0, pl.ds(row0, C), cols], v=v_ref[0, pl.ds(row0, C), cols], kT=kT_ref[0, h, c],
                    q=q_ref[0, pl.ds(row0, C), cols] if with_q else None,
                    a_row=ab_ref[0, h, d, pl.ds(c, 1), :], b_row=ab_ref[0, h, 2 + d, pl.ds(c, 1), :],
                    a_scale=jnp.exp(alog_ref[d, head]), dt_b=dtb_ref[d, head], S=s_ref[2 * h + d], d=d))
                where.append((row0, cols))
        s_new, o = _dn_chunks(ch, consts, with_q)
        for i, (row0, cols) in enumerate(where):
            s_ref[i] = s_new[i]
            if with_q:
                acc_ref[pl.ds(row0, C), cols] += o[i]

    @pl.loop(0, n_ctx)
    def _(n):
        run_chunk(n_lat + n, n_lat + n_ctx - 1 - n, False)

    @pl.loop(0, n_lat)
    def _(n):
        run_chunk(n, n_lat - 1 - n, True)

    rows = 256 if (n_lat * C) % 256 == 0 else C

    @pl.loop(0, (n_lat * C) // rows)
    def _(i):
        r0 = pl.multiple_of(i * rows, rows)
        for h in range(hb):
            cols = slice(h * 128, (h + 1) * 128)
            od = acc_ref[pl.ds(r0, rows), cols]
            zz = z_ref[0, pl.ds(r0, rows), cols].astype(F32)
            o_ref[0, pl.ds(r0, rows), cols] = (_rms(od, g_ref[...]) * zz).astype(o_ref.dtype)


def _deltanet(k, kT, v, q, ab, z, a_log, dt_bias, g_dn):
    B, S, _ = k.shape
    T = q.shape[1]
    C = DN_CHUNK
    assert C == DN_DK == DN_DV == V7X_LANES
    hb = DN_HEADS_PER_STEP
    n_lat, n_ctx = T // C, (S - T) // C
    consts = list(_dn_consts())
    smem = pl.BlockSpec(memory_space=pltpu.SMEM)

    def full(a):
        nd = a.ndim
        return pl.BlockSpec(a.shape, lambda b, g, _n=nd: (0,) * _n)

    return pl.pallas_call(
        functools.partial(_dn_body, hb=hb, n_lat=n_lat, n_ctx=n_ctx),
        out_shape=jax.ShapeDtypeStruct((B, T, DN_V), BF16),
        grid=(B, DN_HEADS // hb),
        in_specs=[smem, smem,
                  pl.BlockSpec((1, S, hb * 128), lambda b, g: (b, 0, g)),
                  pl.BlockSpec((1, hb, S // C, 128, C), lambda b, g: (b, g, 0, 0, 0)),
                  pl.BlockSpec((1, S, hb * 128), lambda b, g: (b, 0, g)),
                  pl.BlockSpec((1, T, hb * 128), lambda b, g: (b, 0, g)),
                  pl.BlockSpec((1, hb, 4, S // C, C), lambda b, g: (b, g, 0, 0, 0)),
                  pl.BlockSpec((1, T, hb * 128), lambda b, g: (b, 0, g)),
                  pl.BlockSpec((1, 128), lambda b, g: (0, 0))] + [full(a) for a in consts],
        out_specs=pl.BlockSpec((1, T, hb * 128), lambda b, g: (b, 0, g)),
        scratch_shapes=[pltpu.VMEM((T, hb * 128), F32), pltpu.VMEM((2 * hb, DN_DK, DN_DV), F32)],
        compiler_params=_cparams(2), name="deltanet",
    )(a_log, dt_bias, k, kT, v, q, ab, z, g_dn, *consts)


def _merge_body(a_ref, b_ref, wa_ref, wb_ref, ga_ref, gb_ref, o_ref):
    oa = _bdot(a_ref[0], wa_ref[...])
    ob = _bdot(b_ref[0], wb_ref[...])
    o_ref[0] = (ga_ref[0].astype(F32) * oa + gb_ref[0].astype(F32) * ob).astype(o_ref.dtype)


def _merge(att, ob, w_oa, w_ob, gates):
    B, T, Ka = att.shape
    D = w_oa.shape[1]
    tm = _pick_tile(T, 1024, 8)
    tn = _pick_tile(D, 512)
    nj = D // tn
    return pl.pallas_call(
        _merge_body,
        out_shape=jax.ShapeDtypeStruct((B, T, D), BF16),
        grid=(B, T // tm, nj),
        in_specs=[pl.BlockSpec((1, tm, Ka), lambda b, i, j: (b, i, 0)),
                  pl.BlockSpec((1, tm, Ka), lambda b, i, j: (b, i, 0)),
                  pl.BlockSpec((Ka, tn), lambda b, i, j: (0, j)),
                  pl.BlockSpec((Ka, tn), lambda b, i, j: (0, j)),
                  pl.BlockSpec((1, tm, tn), lambda b, i, j: (b, i, j)),
                  pl.BlockSpec((1, tm, tn), lambda b, i, j: (b, i, j + nj))],
        out_specs=pl.BlockSpec((1, tm, tn), lambda b, i, j: (b, i, j)),
        compiler_params=_cparams(3), name="merge",
    )(att, ob, w_oa, w_ob, gates, gates)


def _norm_router_body(x_ref, g_ref, sh_ref, sc_ref, wh_ref, wl_ref, br_ref, h_ref, id_ref, gate_ref):
    h = _rms(x_ref[0], g_ref[...]) * (1.0 + sc_ref[0]) + sh_ref[0]
    h_ref[0] = h.astype(h_ref.dtype)
    hh = h.astype(BF16)
    hl = (h - hh.astype(F32)).astype(BF16)
    logits = (_bdot(hh, wh_ref[...]) + _bdot(hl, wh_ref[...]) + _bdot(hh, wl_ref[...])) + br_ref[...]
    lane = lax.broadcasted_iota(jnp.int32, logits.shape, 1)
    neg = jnp.float32(-jnp.inf)
    big = jnp.int32(1 << 20)
    is_g = lane < N_GROUPS
    lg = jnp.where(is_g, logits, neg)
    mg = jnp.max(lg, axis=-1, keepdims=True)
    eg = jnp.where(is_g, jnp.exp(lg - mg), 0.0)
    pg = eg / jnp.sum(eg, axis=-1, keepdims=True)
    pg_top = jnp.max(pg, axis=-1, keepdims=True)
    g_top = jnp.min(jnp.where(jnp.logical_and(is_g, pg == pg_top), lane, big), axis=-1, keepdims=True)
    lo = N_GROUPS + EXPERTS_PER_GROUP * g_top
    is_e = jnp.logical_and(lane >= lo, lane < lo + EXPERTS_PER_GROUP)
    le = jnp.where(is_e, logits, neg)
    me = jnp.max(le, axis=-1, keepdims=True)
    ee = jnp.where(is_e, jnp.exp(le - me), 0.0)
    pe = ee / jnp.sum(ee, axis=-1, keepdims=True)
    p1 = jnp.max(pe, axis=-1, keepdims=True)
    i1 = jnp.min(jnp.where(jnp.logical_and(is_e, pe == p1), lane, big), axis=-1, keepdims=True)
    rest = jnp.logical_and(is_e, lane != i1)
    pe2 = jnp.where(rest, pe, -1.0)
    p2 = jnp.max(pe2, axis=-1, keepdims=True)
    i2 = jnp.min(jnp.where(jnp.logical_and(rest, pe2 == p2), lane, big), axis=-1, keepdims=True)
    den = p1 + p2
    id_ref[0] = jnp.where(lane == 0, i1 - N_GROUPS, jnp.where(lane == 1, i2 - N_GROUPS, 0))
    gate_ref[0] = jnp.where(lane == 0, p1 / den * pg_top, jnp.where(lane == 1, p2 / den * pg_top, 0.0))


def _norm_router(x, g, shift, scale, w_hi, w_lo, b_r):
    B, T, D = x.shape
    tr = _pick_tile(T, 256, 8)
    return pl.pallas_call(
        _norm_router_body,
        out_shape=(jax.ShapeDtypeStruct((B, T, D), F32),
                   jax.ShapeDtypeStruct((B, T, 128), jnp.int32),
                   jax.ShapeDtypeStruct((B, T, 128), F32)),
        grid=(B, T // tr),
        in_specs=[pl.BlockSpec((1, tr, D), lambda b, i: (b, i, 0)),
                  pl.BlockSpec((1, D), lambda b, i: (0, 0)),
                  pl.BlockSpec((1, 1, D), lambda b, i: (b, 0, 0)),
                  pl.BlockSpec((1, 1, D), lambda b, i: (b, 0, 0)),
                  pl.BlockSpec((D, 128), lambda b, i: (0, 0)),
                  pl.BlockSpec((D, 128), lambda b, i: (0, 0)),
                  pl.BlockSpec((1, 128), lambda b, i: (0, 0))],
        out_specs=(pl.BlockSpec((1, tr, D), lambda b, i: (b, i, 0)),
                   pl.BlockSpec((1, tr, 128), lambda b, i: (b, i, 0)),
                   pl.BlockSpec((1, tr, 128), lambda b, i: (b, i, 0))),
        compiler_params=_cparams(2), name="norm_router",
    )(x, g, shift, scale, w_hi, w_lo, b_r)


def _moe_gather_body(nu_ref, src_ref, h_ref, o_ref, buf_ref, sem):
    i = pl.program_id(0)

    def row_copy(r, s):
        return pltpu.make_async_copy(h_ref.at[pl.ds(s, 1)], buf_ref.at[pl.ds(r, 1)], sem)

    @pl.when(i < nu_ref[0])
    def _():
        def start(r, c):
            row_copy(r, src_ref[0, 0, r]).start()
            return c

        def wait(r, c):
            row_copy(r, 0).wait()
            return c

        lax.fori_loop(0, MOE_ROWS, start, 0, unroll=8)
        lax.fori_loop(0, MOE_ROWS, wait, 0, unroll=8)
        o_ref[...] = buf_ref[...].astype(o_ref.dtype)

    @pl.when(i >= nu_ref[0])
    def _():
        o_ref[...] = jnp.zeros(o_ref.shape, o_ref.dtype)


def _moe_gather(h, src, n_used):
    n, D = h.shape
    R = src.shape[0]
    nb = R // MOE_ROWS
    return pl.pallas_call(
        _moe_gather_body,
        out_shape=jax.ShapeDtypeStruct((R, D), BF16),
        grid_spec=pltpu.PrefetchScalarGridSpec(
            num_scalar_prefetch=1, grid=(nb,),
            in_specs=[pl.BlockSpec((1, 1, MOE_ROWS), lambda i, nu: (i, 0, 0), memory_space=pltpu.SMEM),
                      pl.BlockSpec(memory_space=pl.ANY)],
            out_specs=pl.BlockSpec((MOE_ROWS, D), lambda i, nu: (i, 0)),
            scratch_shapes=[pltpu.VMEM((MOE_ROWS, D), F32), pltpu.SemaphoreType.DMA(())]),
        compiler_params=_cparams(1), name="moe_gather",
    )(n_used, src.reshape(nb, 1, MOE_ROWS), h)


def _moe_up_body(be_ref, nu_ref, x_ref, w1_ref, w3_ref, o_ref, w1b_ref, w3b_ref):
    i = pl.program_id(1)
    changed = jnp.logical_or(i == 0, be_ref[i] != be_ref[jnp.maximum(i - 1, 0)])

    @pl.when(changed)
    def _():
        w1b_ref[...] = w1_ref[0].astype(BF16)
        w3b_ref[...] = w3_ref[0].astype(BF16)

    @pl.when(i < nu_ref[0])
    def _():
        x = x_ref[...]
        a = _bdot(x, w1b_ref[...])
        b = _bdot(x, w3b_ref[...])
        o_ref[...] = (_silu(a) * b).astype(o_ref.dtype)

    @pl.when(i >= nu_ref[0])
    def _():
        o_ref[...] = jnp.zeros(o_ref.shape, o_ref.dtype)


def _moe_down_body(be_ref, nu_ref, h_ref, w2_ref, o_ref, w2b_ref):
    i = pl.program_id(1)
    changed = jnp.logical_or(i == 0, be_ref[i] != be_ref[jnp.maximum(i - 1, 0)])

    @pl.when(changed)
    def _():
        w2b_ref[...] = w2_ref[0].astype(BF16)

    @pl.when(i < nu_ref[0])
    def _():
        o_ref[...] = _bdot(h_ref[...], w2b_ref[...]).astype(o_ref.dtype)

    @pl.when(i >= nu_ref[0])
    def _():
        o_ref[...] = jnp.zeros(o_ref.shape, o_ref.dtype)


def _moe_experts(x_pad, blk_expert, n_used, w1, w3, w2):
    R, D = x_pad.shape
    nb = R // MOE_ROWS
    Hd = w1.shape[2]
    th = _pick_tile(Hd, 512)
    hbuf = pl.pallas_call(
        _moe_up_body,
        out_shape=jax.ShapeDtypeStruct((R, Hd), BF16),
        grid_spec=pltpu.PrefetchScalarGridSpec(
            num_scalar_prefetch=2, grid=(Hd // th, nb),
            in_specs=[pl.BlockSpec((MOE_ROWS, D), lambda j, i, be, nu: (i, 0)),
                      pl.BlockSpec((1, D, th), lambda j, i, be, nu: (be[i], 0, j)),
                      pl.BlockSpec((1, D, th), lambda j, i, be, nu: (be[i], 0, j))],
            out_specs=pl.BlockSpec((MOE_ROWS, th), lambda j, i, be, nu: (i, j)),
            scratch_shapes=[pltpu.VMEM((D, th), BF16), pltpu.VMEM((D, th), BF16)]),
        compiler_params=_cparams(2), name="moe_up",
    )(blk_expert, n_used, x_pad, w1, w3)
    tn = _pick_tile(D, 2048)
    return pl.pallas_call(
        _moe_down_body,
        out_shape=jax.ShapeDtypeStruct((R, D), F32),
        grid_spec=pltpu.PrefetchScalarGridSpec(
            num_scalar_prefetch=2, grid=(D // tn, nb),
            in_specs=[pl.BlockSpec((MOE_ROWS, Hd), lambda j, i, be, nu: (i, 0)),
                      pl.BlockSpec((1, Hd, tn), lambda j, i, be, nu: (be[i], 0, j))],
            out_specs=pl.BlockSpec((MOE_ROWS, tn), lambda j, i, be, nu: (i, j)),
            scratch_shapes=[pltpu.VMEM((Hd, tn), BF16)]),
        compiler_params=_cparams(2), name="moe_down",
    )(blk_expert, n_used, hbuf, w2)


def _combine_body(dest_ref, x_ref, p_ref, g_ref, y_ref, o_ref, buf_ref, sem, *, tr):
    def row_copy(r, s):
        return pltpu.make_async_copy(y_ref.at[pl.ds(s, 1)], buf_ref.at[pl.ds(r, 1)], sem)

    def start(r, c):
        row_copy(r, dest_ref[0, 0, r]).start()
        return c

    def wait(r, c):
        row_copy(r, 0).wait()
        return c

    lax.fori_loop(0, TOP_K * tr, start, 0, unroll=8)
    lax.fori_loop(0, TOP_K * tr, wait, 0, unroll=8)
    p = p_ref[0]
    ffn = p[:, 0:1] * buf_ref[0:tr, :] + p[:, 1:2] * buf_ref[tr:2 * tr, :]
    o_ref[0] = x_ref[0] + g_ref[0] * ffn


def _combine(x, y_pad, dest, gate, g2):
    B, T, D = x.shape
    tr = _pick_tile(T, 256, 8)
    nt = T // tr
    dest_t = dest.reshape(B * nt, tr, TOP_K).transpose(0, 2, 1).reshape(B * nt, 1, TOP_K * tr)
    row = pl.BlockSpec((1, tr, D), lambda b, i: (b, i, 0))
    return pl.pallas_call(
        functools.partial(_combine_body, tr=tr),
        out_shape=jax.ShapeDtypeStruct((B, T, D), F32),
        grid=(B, nt),
        in_specs=[pl.BlockSpec((1, 1, TOP_K * tr), lambda b, i: (b * nt + i, 0, 0), memory_space=pltpu.SMEM),
                  row,
                  pl.BlockSpec((1, tr, 128), lambda b, i: (b, i, 0)),
                  pl.BlockSpec((1, 1, D), lambda b, i: (b, 0, 0)),
                  pl.BlockSpec(memory_space=pl.ANY)],
        out_specs=row,
        scratch_shapes=[pltpu.VMEM((TOP_K * tr, D), F32), pltpu.SemaphoreType.DMA(())],
        compiler_params=_cparams(2), name="moe_combine",
    )(dest_t, x, gate, g2, y_pad)


def _rope_tables(T, Tc):
    rows = T // GRID_W
    row = jnp.repeat(jnp.arange(rows), GRID_W).astype(F32)
    col = jnp.tile(jnp.arange(GRID_W), rows).astype(F32)
    half = QK_ROPE // 2
    inv_freq = ROPE_THETA ** (-jnp.arange(0, half, 2, dtype=F32) / half)
    ang = jnp.concatenate([row[:, None] * inv_freq, col[:, None] * inv_freq], axis=-1)
    cos, sin = jnp.cos(ang), jnp.sin(ang)
    z64 = jnp.zeros((T, 64), F32)
    lat = jnp.concatenate([cos, cos, z64, -sin, sin, z64], axis=-1)
    ctx = jnp.concatenate([jnp.ones((Tc, 64), F32), jnp.zeros((Tc, 192), F32)], axis=-1)
    return jnp.concatenate([lat, ctx], axis=0)


_PERM = np.concatenate([np.arange(0, QK_ROPE, 2), np.arange(1, QK_ROPE, 2)])
_PERM_SW = np.concatenate([np.arange(1, QK_ROPE, 2), np.arange(0, QK_ROPE, 2)])


def _pad_cols(a, n):
    return jnp.pad(a, ((0, 0), (0, n - a.shape[1])))


def _moe_dispatch(expert_ids, n_tokens):
    a = n_tokens * TOP_K
    e_flat = expert_ids.reshape(a)
    onehot = (e_flat[:, None] == jnp.arange(N_EXPERTS)[None, :]).astype(jnp.int32)
    csum = jnp.cumsum(onehot, axis=0)
    rank = jnp.take_along_axis(csum, e_flat[:, None], axis=1)[:, 0] - 1
    counts = csum[-1]
    padded = (counts + MOE_ROWS - 1) // MOE_ROWS * MOE_ROWS
    pad_end = jnp.cumsum(padded)
    pad_start = pad_end - padded
    dest = pad_start[e_flat] + rank
    n_blocks = -(-a // MOE_ROWS) + N_EXPERTS
    src = jnp.zeros((n_blocks * MOE_ROWS,), jnp.int32).at[dest].set(jnp.arange(a, dtype=jnp.int32) // TOP_K)
    blk_expert = jnp.minimum(
        jnp.searchsorted(pad_end, jnp.arange(n_blocks) * MOE_ROWS, side="right"), N_EXPERTS - 1).astype(jnp.int32)
    n_used = (pad_end[-1] // MOE_ROWS).astype(jnp.int32).reshape(1)
    return dest, src, blk_expert, n_used


def _layer(x, c, ctx, c_ctx, p):
    B, T, D = x.shape
    Tc = ctx.shape[1]
    S = T + Tc
    H = MLA_HEADS

    cin = jnp.zeros((1, 16, D), F32).at[0, :B].set(c).at[0, B].set(c_ctx)
    mod = _matmul(cin, p["w_mod"], tm=16, tn=1024, out_dtype=F32, pre=_silu,
                  epi=lambda acc, b: acc + b, extras=[(p["b_mod"].reshape(1, 1, -1), "bcol")], name="mod")[0]
    mod_l = mod[:B].reshape(B, N_MOD, 1, D)
    mod_c = mod[B].reshape(N_MOD, 1, 1, D)

    h = _norm_mod(x, ctx, p["norm1_g"].reshape(1, D), mod_l[:, 0], mod_l[:, 1], mod_c[0], mod_c[1])

    w_in = p["w_in"]
    wkr = w_in[:, OFF_KR:OFF_AB]
    w_small = jnp.concatenate([w_in[:, OFF_CKV:OFF_KR], _pad_cols(wkr[:, _PERM], 128),
                               _pad_cols(wkr[:, _PERM_SW], 128), _pad_cols(w_in[:, OFF_AB:OFF_DK], 128)],
                              axis=1).astype(BF16)
    p_small = _matmul(h, w_small, tm=1152, tn=896, out_dtype=F32, name="in_small")
    p_dkv = _matmul(h, w_in[:, OFF_DK:OFF_DQ].astype(BF16), tm=1152, name="in_dkv")
    p_dq = _matmul(h, w_in[:, OFF_DQ:OFF_CQ].astype(BF16), rows=T, name="in_dq")
    p_cq = _matmul(h, w_in[:, OFF_CQ:OFF_Z].astype(BF16), rows=T, out_dtype=F32, name="in_cq")
    z_act = _matmul(h, w_in[:, OFF_Z:OFF_GATE].astype(BF16), rows=T, epi=_silu, name="in_z")
    gates = _matmul(h, w_in[:, OFF_GATE:].astype(BF16), rows=T, epi=jax.nn.sigmoid, name="in_gate")

    tab = _rope_tables(T, Tc)
    gkr = p["k_rope_norm_g"]
    ckv_n, k_rope = _mla_small(p_small, tab, p["kv_norm_g"].reshape(1, -1),
                               _pad_cols(gkr[_PERM][None], 128), _pad_cols(gkr[_PERM_SW][None], 128))
    w_ukv = p["w_ukv"].reshape(KV_LORA, H, QK_NOPE + V_HEAD)
    hg = HEAD_GROUP
    w_kv_perm = jnp.concatenate(
        [w_ukv[:, :, :QK_NOPE].reshape(KV_LORA, H // hg, hg * QK_NOPE),
         w_ukv[:, :, QK_NOPE:].reshape(KV_LORA, H // hg, hg * V_HEAD)], axis=2).reshape(KV_LORA, H * 256).astype(BF16)
    k_all, v_all = _kv_up(ckv_n, w_kv_perm, k_rope, p["k_norm_g"].reshape(1, -1))

    w_uq = p["w_uq"].reshape(Q_LORA, H, QK_NOPE + QK_ROPE)
    wq_r = w_uq[:, :, QK_NOPE:]
    zpad = jnp.zeros((Q_LORA, H, 64), F32)
    w_q_perm = jnp.concatenate(
        [w_uq[:, :, :QK_NOPE].reshape(Q_LORA, H // hg, hg * 128),
         jnp.concatenate([wq_r[:, :, _PERM], zpad], axis=2).reshape(Q_LORA, H // hg, hg * 128),
         jnp.concatenate([wq_r[:, :, _PERM_SW], zpad], axis=2).reshape(Q_LORA, H // hg, hg * 128)],
        axis=2).reshape(Q_LORA, H * 384).astype(BF16)
    gqr = p["q_rope_norm_g"]
    q_all = _q_up(p_cq, w_q_perm, tab[:T], p["q_norm_g"].reshape(1, -1), p["q_a_norm_g"].reshape(1, -1),
                  _pad_cols(gqr[_PERM][None], 128), _pad_cols(gqr[_PERM_SW][None], 128))
    att = _attention(q_all, k_all, v_all)

    cw = jnp.pad(p["conv_w"], ((0, 8 - CONV_W), (0, 0)))
    dn_k = _short_conv(p_dkv, 0, DN_QK, cw[:, :DN_QK], n_lat_rows=T, l2=True)
    dn_v = _short_conv(p_dkv, DN_QK, DN_V, cw[:, DN_QK:DN_QK + DN_V], n_lat_rows=T, l2=False)
    dn_q = _short_conv(p_dq, 0, DN_QK, cw[:, DN_QK + DN_V:], n_lat_rows=T, l2=True, scale=float(DN_DK ** -0.5))
    C = DN_CHUNK
    dn_kT = dn_k.reshape(B, S // C, C, DN_HEADS, DN_DK).transpose(0, 3, 1, 4, 2)
    ab = p_small[:, :, 768:768 + 4 * DN_HEADS].reshape(B, S // C, C, 4, DN_HEADS).transpose(0, 4, 3, 1, 2)
    ob_in = _deltanet(dn_k, dn_kT, dn_v, dn_q, ab, z_act, p["a_log"], p["dt_bias"], p["dn_norm_g"].reshape(1, -1))

    mix = _merge(att, ob_in, p["w_oa"].astype(BF16), p["w_ob"].astype(BF16), gates)
    xl = _matmul(mix, p["w_out"].astype(BF16), out_dtype=F32,
                 epi=lambda acc, xr, g: xr + g * acc, extras=[(x, "tile"), (mod_l[:, 2], "bcol")], name="out_proj")

    w_r = _pad_cols(jnp.concatenate([p["w_rg"], p["w_re"]], axis=1), 128)
    w_r_hi = w_r.astype(BF16)
    w_r_lo = (w_r - w_r_hi.astype(F32)).astype(BF16)
    b_r = _pad_cols(jnp.concatenate([p["b_rg"], p["b_re"]])[None], 128)
    h2, ids, gate = _norm_router(xl, p["norm2_g"].reshape(1, D), mod_l[:, 3], mod_l[:, 4], w_r_hi, w_r_lo, b_r)
    n = B * T
    dest, src, blk_expert, n_used = _moe_dispatch(ids[:, :, :TOP_K].reshape(n, TOP_K), n)
    x_pad = _moe_gather(h2.reshape(n, D), src, n_used)
    y_pad = _moe_experts(x_pad, blk_expert, n_used, p["w1"], p["w3"], p["w2"])
    return _combine(xl, y_pad, dest.reshape(n, TOP_K), gate, mod_l[:, 5])


def kernel(x, c, ctx, c_ctx, norm1_g, norm2_g, w_mod, b_mod, w_in, q_a_norm_g, w_uq, kv_norm_g, w_ukv, q_norm_g,
           q_rope_norm_g, k_norm_g, k_rope_norm_g, conv_w, a_log, dt_bias, dn_norm_g, w_oa, w_ob, w_out, w_rg, b_rg,
           w_re, b_re, w1, w3, w2):
    depth = norm1_g.shape[0]
    assert depth == 1, "single-layer block: the context stream is read, never updated"
    layer = 0
    p = {
        "norm1_g": norm1_g[layer], "norm2_g": norm2_g[layer], "w_mod": w_mod[layer], "b_mod": b_mod[layer],
        "w_in": w_in[layer], "q_a_norm_g": q_a_norm_g[layer], "w_uq": w_uq[layer], "kv_norm_g": kv_norm_g[layer],
        "w_ukv": w_ukv[layer], "q_norm_g": q_norm_g[layer], "q_rope_norm_g": q_rope_norm_g[layer],
        "k_norm_g": k_norm_g[layer], "k_rope_norm_g": k_rope_norm_g[layer], "conv_w": conv_w[layer],
        "a_log": a_log[layer], "dt_bias": dt_bias[layer], "dn_norm_g": dn_norm_g[layer], "w_oa": w_oa[layer],
        "w_ob": w_ob[layer], "w_out": w_out[layer], "w_rg": w_rg[layer], "b_rg": b_rg[layer], "w_re": w_re[layer],
        "b_re": b_re[layer], "w1": w1[layer], "w3": w3[layer], "w2": w2[layer],
    }
    return _layer(x, c, ctx, c_ctx, p)
```

```python
import functools

import numpy as np
import jax
import jax.numpy as jnp
from jax import lax
from jax.experimental import pallas as pl
from jax.experimental.pallas import tpu as pltpu

F32 = jnp.float32
BF16 = jnp.bfloat16

EPS = 1e-6
GRID_W = 64
N_MOD = 6
MLA_HEADS = 16
Q_LORA = 1024
KV_LORA = 512
QK_NOPE = 128
QK_ROPE = 64
V_HEAD = 128
ROPE_THETA = 10000.0
DN_HEADS = 16
DN_DK = 128
DN_DV = 128
DN_QK = DN_HEADS * DN_DK
DN_V = DN_HEADS * DN_DV
CONV_W = 5
N_GROUPS = 4
EXPERTS_PER_GROUP = 8
N_EXPERTS = N_GROUPS * EXPERTS_PER_GROUP
TOP_K = 2
EXPERT_HIDDEN = 1024

OFF_CKV = 0
OFF_KR = OFF_CKV + KV_LORA
OFF_AB = OFF_KR + QK_ROPE
OFF_DK = OFF_AB + 4 * DN_HEADS
OFF_DV = OFF_DK + DN_QK
OFF_DQ = OFF_DV + DN_V
OFF_CQ = OFF_DQ + DN_QK
OFF_Z = OFF_CQ + Q_LORA
OFF_GATE = OFF_Z + DN_V

V7X_LANES = 128
V7X_VMEM_BYTES = 64 * 1024 * 1024
VMEM_LIMIT = V7X_VMEM_BYTES - 8 * 1024 * 1024

DN_CHUNK = 128
MOE_ROWS = 512
HEAD_GROUP = 4
DN_HEADS_PER_STEP = 8
ATTN_ROWS = 1024
ATTN_SPLIT = 4
CONV_HALO = 16


def _cparams(n_grid):
    return pltpu.CompilerParams(dimension_semantics=("arbitrary",) * n_grid, vmem_limit_bytes=VMEM_LIMIT)


def _pick_tile(n, target, mult=V7X_LANES):
    if n <= target:
        return n
    best = None
    t = mult
    while t <= target:
        if n % t == 0:
            best = t
        t += mult
    assert best is not None, (n, target)
    return best


def _rms(x, g):
    return x * lax.rsqrt(jnp.mean(x * x, axis=-1, keepdims=True) + EPS) * g


def _silu(x):
    return x * jax.nn.sigmoid(x)


def _bdot(a, b):
    return jnp.dot(a, b, preferred_element_type=F32)


def _mm_body(*refs, pre, epi, n_extra, cached):
    x_ref, w_ref = refs[0], refs[1]
    extra = refs[2:2 + n_extra]
    o_ref = refs[2 + n_extra]
    if cached:
        xs_ref = refs[3 + n_extra]

        @pl.when(pl.program_id(2) == 0)
        def _():
            xs_ref[...] = pre(x_ref[0]).astype(BF16)

        x = xs_ref[...]
    else:
        x = x_ref[0]
    w = w_ref[...]
    if w.dtype != BF16:
        w = w.astype(BF16)
    acc = _bdot(x, w)
    if epi is not None:
        acc = epi(acc, *[e[0] for e in extra])
    o_ref[0] = acc.astype(o_ref.dtype)


def _matmul(x, w, *, rows=None, tm=1024, tn=512, out_dtype=BF16, pre=None, epi=None, extras=(), name="mm"):
    B, S, K = x.shape
    N = w.shape[1]
    rows = S if rows is None else rows
    tm = _pick_tile(rows, tm)
    tn = _pick_tile(N, tn)
    grid = (B, rows // tm, N // tn)
    in_specs = [pl.BlockSpec((1, tm, K), lambda b, i, j: (b, i, 0)),
                pl.BlockSpec((K, tn), lambda b, i, j: (0, j))]
    args = [x, w]
    for arr, kind in extras:
        if kind == "tile":
            in_specs.append(pl.BlockSpec((1, tm, tn), lambda b, i, j: (b, i, j)))
        elif kind == "bcol":
            if arr.shape[0] == 1:
                in_specs.append(pl.BlockSpec((1, 1, tn), lambda b, i, j: (0, 0, j)))
            else:
                in_specs.append(pl.BlockSpec((1, 1, tn), lambda b, i, j: (b, 0, j)))
        else:
            raise ValueError(kind)
        args.append(arr)
    cached = pre is not None
    scratch = [pltpu.VMEM((tm, K), BF16)] if cached else []
    if not cached:
        assert x.dtype == BF16
    return pl.pallas_call(
        functools.partial(_mm_body, pre=pre, epi=epi, n_extra=len(extras), cached=cached),
        out_shape=jax.ShapeDtypeStruct((B, rows, N), out_dtype),
        grid=grid, in_specs=in_specs,
        out_specs=pl.BlockSpec((1, tm, tn), lambda b, i, j: (b, i, j)),
        scratch_shapes=scratch, compiler_params=_cparams(3), name=name,
    )(*args)


def _norm_mod_body(x_ref, c_ref, g_ref, sl_ref, scl_ref, sc_ref, scc_ref, o_ref, *, n_lat):
    j = pl.program_id(1)

    @pl.when(j < n_lat)
    def _():
        h = _rms(x_ref[0], g_ref[...])
        o_ref[0] = (h * (1.0 + scl_ref[0]) + sl_ref[0]).astype(o_ref.dtype)

    @pl.when(j >= n_lat)
    def _():
        h = _rms(c_ref[0], g_ref[...])
        o_ref[0] = (h * (1.0 + scc_ref[0]) + sc_ref[0]).astype(o_ref.dtype)


def _norm_mod(x, ctx, g, shift_l, scale_l, shift_c, scale_c):
    B, T, D = x.shape
    Tc = ctx.shape[1]
    tr = _pick_tile(int(np.gcd(T, Tc)), 256, 8)
    n_lat, n_ctx = T // tr, Tc // tr
    return pl.pallas_call(
        functools.partial(_norm_mod_body, n_lat=n_lat),
        out_shape=jax.ShapeDtypeStruct((B, T + Tc, D), BF16),
        grid=(B, n_lat + n_ctx),
        in_specs=[pl.BlockSpec((1, tr, D), lambda b, j: (b, jnp.minimum(j, n_lat - 1), 0)),
                  pl.BlockSpec((1, tr, D), lambda b, j: (b, jnp.maximum(j - n_lat, 0), 0)),
                  pl.BlockSpec((1, D), lambda b, j: (0, 0)),
                  pl.BlockSpec((1, 1, D), lambda b, j: (b, 0, 0)),
                  pl.BlockSpec((1, 1, D), lambda b, j: (b, 0, 0)),
                  pl.BlockSpec((1, 1, D), lambda b, j: (0, 0, 0)),
                  pl.BlockSpec((1, 1, D), lambda b, j: (0, 0, 0))],
        out_specs=pl.BlockSpec((1, tr, D), lambda b, j: (b, j, 0)),
        compiler_params=_cparams(2), name="norm_mod",
    )(x, ctx, g, shift_l, scale_l, shift_c, scale_c)


def _mla_small_body(p_ref, tab_ref, gkv_ref, gkr_ref, gks_ref, ckv_ref, kr_ref):
    p = p_ref[0]
    ckv_ref[0] = _rms(p[:, :KV_LORA], gkv_ref[...]).astype(ckv_ref.dtype)
    xr = p[:, KV_LORA:KV_LORA + 128]
    xs = p[:, KV_LORA + 128:KV_LORA + 256]
    inv = lax.rsqrt(jnp.sum(xr * xr, axis=-1, keepdims=True) * (1.0 / QK_ROPE) + EPS)
    tab = tab_ref[...]
    kr = inv * (xr * gkr_ref[...] * tab[:, :128] + xs * gks_ref[...] * tab[:, 128:])
    kr_ref[0] = kr.astype(kr_ref.dtype)


def _mla_small(p_small, tab, g_kv, g_kr, g_ks):
    B, S, W = p_small.shape
    tr = _pick_tile(S, 768, 8)
    return pl.pallas_call(
        _mla_small_body,
        out_shape=(jax.ShapeDtypeStruct((B, S, KV_LORA), BF16), jax.ShapeDtypeStruct((B, S, 128), BF16)),
        grid=(B, S // tr),
        in_specs=[pl.BlockSpec((1, tr, 768), lambda b, i: (b, i, 0)),
                  pl.BlockSpec((tr, 256), lambda b, i: (i, 0)),
                  pl.BlockSpec((1, KV_LORA), lambda b, i: (0, 0)),
                  pl.BlockSpec((1, 128), lambda b, i: (0, 0)),
                  pl.BlockSpec((1, 128), lambda b, i: (0, 0))],
        out_specs=(pl.BlockSpec((1, tr, KV_LORA), lambda b, i: (b, i, 0)),
                   pl.BlockSpec((1, tr, 128), lambda b, i: (b, i, 0))),
        compiler_params=_cparams(2), name="mla_small",
    )(p_small, tab, g_kv, g_kr, g_ks)


def _kv_up_body(x_ref, w_ref, kr_ref, gk_ref, k_ref, v_ref, *, hg):
    acc = _bdot(x_ref[0], w_ref[...])
    kr = kr_ref[0]
    for h in range(hg):
        kn = _rms(acc[:, h * 128:(h + 1) * 128], gk_ref[...])
        k_ref[0, :, h * 256:h * 256 + 128] = kn.astype(k_ref.dtype)
        k_ref[0, :, h * 256 + 128:(h + 1) * 256] = kr
    v_ref[0] = acc[:, hg * 128:].astype(v_ref.dtype)


def _kv_up(ckv, w_perm, kr, g_k):
    B, S, _ = ckv.shape
    hg = HEAD_GROUP
    tm = _pick_tile(S, 1152, 8)
    return pl.pallas_call(
        functools.partial(_kv_up_body, hg=hg),
        out_shape=(jax.ShapeDtypeStruct((B, S, MLA_HEADS * 256), BF16),
                   jax.ShapeDtypeStruct((B, S, MLA_HEADS * V_HEAD), BF16)),
        grid=(B, S // tm, MLA_HEADS // hg),
        in_specs=[pl.BlockSpec((1, tm, KV_LORA), lambda b, i, j: (b, i, 0)),
                  pl.BlockSpec((KV_LORA, hg * 256), lambda b, i, j: (0, j)),
                  pl.BlockSpec((1, tm, 128), lambda b, i, j: (b, i, 0)),
                  pl.BlockSpec((1, 128), lambda b, i, j: (0, 0))],
        out_specs=(pl.BlockSpec((1, tm, hg * 256), lambda b, i, j: (b, i, j)),
                   pl.BlockSpec((1, tm, hg * 128), lambda b, i, j: (b, i, j))),
        compiler_params=_cparams(3), name="kv_up",
    )(ckv, w_perm, kr, g_k)


def _q_up_body(x_ref, w_ref, tab_ref, gq_ref, gqa_ref, gr_ref, gs_ref, q_ref, xs_ref, *, hg, scale):
    @pl.when(pl.program_id(2) == 0)
    def _():
        xs_ref[...] = _rms(x_ref[0], gqa_ref[...]).astype(BF16)

    acc = _bdot(xs_ref[...], w_ref[...])
    tab = tab_ref[...]
    for h in range(hg):
        qn = _rms(acc[:, h * 128:(h + 1) * 128], gq_ref[...]) * scale
        xr = acc[:, (hg + h) * 128:(hg + h + 1) * 128]
        xw = acc[:, (2 * hg + h) * 128:(2 * hg + h + 1) * 128]
        inv = lax.rsqrt(jnp.sum(xr * xr, axis=-1, keepdims=True) * (1.0 / QK_ROPE) + EPS) * scale
        qr = inv * (xr * gr_ref[...] * tab[:, :128] + xw * gs_ref[...] * tab[:, 128:])
        q_ref[0, :, h * 256:h * 256 + 128] = qn.astype(q_ref.dtype)
        q_ref[0, :, h * 256 + 128:(h + 1) * 256] = qr.astype(q_ref.dtype)


def _q_up(cq, w_perm, tab, g_q, g_qa, g_r, g_s):
    B, T, _ = cq.shape
    hg = HEAD_GROUP
    tm = _pick_tile(T, 512, 8)
    scale = float((QK_NOPE + QK_ROPE) ** -0.5 * np.log2(np.e))
    return pl.pallas_call(
        functools.partial(_q_up_body, hg=hg, scale=scale),
        out_shape=jax.ShapeDtypeStruct((B, T, MLA_HEADS * 256), BF16),
        grid=(B, T // tm, MLA_HEADS // hg),
        in_specs=[pl.BlockSpec((1, tm, Q_LORA), lambda b, i, j: (b, i, 0)),
                  pl.BlockSpec((Q_LORA, hg * 384), lambda b, i, j: (0, j)),
                  pl.BlockSpec((tm, 256), lambda b, i, j: (i, 0)),
                  pl.BlockSpec((1, 128), lambda b, i, j: (0, 0)),
                  pl.BlockSpec((1, Q_LORA), lambda b, i, j: (0, 0)),
                  pl.BlockSpec((1, 128), lambda b, i, j: (0, 0)),
                  pl.BlockSpec((1, 128), lambda b, i, j: (0, 0))],
        out_specs=pl.BlockSpec((1, tm, hg * 256), lambda b, i, j: (b, i, j)),
        scratch_shapes=[pltpu.VMEM((tm, Q_LORA), BF16)],
        compiler_params=_cparams(3), name="q_up",
    )(cq, w_perm, tab, g_q, g_qa, g_r, g_s)


def _attn_body(q_ref, k_ref, v_ref, o_ref, *, n_split):
    k, v = k_ref[0], v_ref[0]
    rows = q_ref.shape[1] // n_split
    sl = [slice(i * rows, (i + 1) * rows) for i in range(n_split)]
    s = [lax.dot_general(q_ref[0, r, :], k, (((1,), (1,)), ((), ())), preferred_element_type=F32) for r in sl]
    m = [jnp.max(x, axis=-1, keepdims=True) for x in s]
    p = [jnp.exp2(x - mx) for x, mx in zip(s, m)]
    l = [jnp.sum(x, axis=-1, keepdims=True) for x in p]
    o = [_bdot(x.astype(BF16), v) for x in p]
    for r, ox, lx in zip(sl, o, l):
        o_ref[0, r, :] = (ox / lx).astype(o_ref.dtype)


def _attention(q, k, v):
    B, T, _ = q.shape
    S = k.shape[1]
    tq = _pick_tile(T, ATTN_ROWS, 8)
    return pl.pallas_call(
        functools.partial(_attn_body, n_split=ATTN_SPLIT if tq % (8 * ATTN_SPLIT) == 0 else 1),
        out_shape=jax.ShapeDtypeStruct((B, T, MLA_HEADS * V_HEAD), BF16),
        grid=(B, MLA_HEADS, T // tq),
        in_specs=[pl.BlockSpec((1, tq, 256), lambda b, h, i: (b, i, h)),
                  pl.BlockSpec((1, S, 256), lambda b, h, i: (b, 0, h)),
                  pl.BlockSpec((1, S, V_HEAD), lambda b, h, i: (b, 0, h))],
        out_specs=pl.BlockSpec((1, tq, V_HEAD), lambda b, h, i: (b, i, h)),
        compiler_params=_cparams(3), name="mla_attention",
    )(q, k, v)


def _conv_body(prev_ref, cur_ref, next_ref, w_ref, o_ref, buf_ref, *, n_lat, n_tiles, l2, scale, tr):
    j = pl.program_id(1)
    tc = cur_ref.shape[2]
    prev_ok = jnp.logical_and(j != 0, j != n_lat)
    next_ok = jnp.logical_and(j != n_lat - 1, j != n_tiles - 1)
    zeros8 = jnp.zeros((8, tc), F32)
    buf_ref[0:8, :] = jnp.where(prev_ok, prev_ref[0].astype(F32)[CONV_HALO - 8:], zeros8)
    buf_ref[8:8 + tr, :] = cur_ref[0].astype(F32)
    buf_ref[8 + tr:16 + tr, :] = jnp.where(next_ok, next_ref[0].astype(F32)[:8], zeros8)
    w = w_ref[...]
    y = jnp.zeros((tr, tc), F32)
    for t in range(CONV_W):
        r0 = 8 + t - CONV_W // 2
        y = y + buf_ref[r0:r0 + tr, :] * w[t:t + 1, :]
    y = _silu(y)
    if l2:
        for h in range(tc // 128):
            yh = y[:, h * 128:(h + 1) * 128]
            yh = yh * lax.rsqrt(jnp.sum(yh * yh, axis=-1, keepdims=True) + EPS) * scale
            o_ref[0, :, h * 128:(h + 1) * 128] = yh.astype(o_ref.dtype)
    else:
        o_ref[0] = y.astype(o_ref.dtype)


def _short_conv(x, col0, width, w8, *, n_lat_rows, l2, scale=1.0):
    B, R, _ = x.shape
    tr = _pick_tile(int(np.gcd(n_lat_rows, R - n_lat_rows)) if R > n_lat_rows else n_lat_rows, 256, CONV_HALO)
    tc = _pick_tile(width, 1024)
    n_tiles, n_lat = R // tr, n_lat_rows // tr
    cb = col0 // tc
    hpt = tr // CONV_HALO
    n_halo = R // CONV_HALO
    assert col0 % tc == 0
    return pl.pallas_call(
        functools.partial(_conv_body, n_lat=n_lat, n_tiles=n_tiles, l2=l2, scale=scale, tr=tr),
        out_shape=jax.ShapeDtypeStruct((B, R, width), BF16),
        grid=(B, n_tiles, width // tc),
        in_specs=[pl.BlockSpec((1, CONV_HALO, tc), lambda b, j, c: (b, jnp.maximum(j * hpt - 1, 0), cb + c)),
                  pl.BlockSpec((1, tr, tc), lambda b, j, c: (b, j, cb + c)),
                  pl.BlockSpec((1, CONV_HALO, tc),
                               lambda b, j, c: (b, jnp.minimum((j + 1) * hpt, n_halo - 1), cb + c)),
                  pl.BlockSpec((8, tc), lambda b, j, c: (0, c))],
        out_specs=pl.BlockSpec((1, tr, tc), lambda b, j, c: (b, j, c)),
        scratch_shapes=[pltpu.VMEM((tr + 16, tc), F32)],
        compiler_params=_cparams(3), name="short_conv",
    )(x, x, x, w8)


def _dn_consts():
    C = DN_CHUNK
    r = np.arange(C)[:, None]
    c = np.arange(C)[None, :]
    tri = np.stack([(c <= r), (c >= r)]).astype(np.float32)
    ones = np.ones((C, C), np.float32)
    rhs = np.stack([np.concatenate([(r > c).astype(np.float32), ones], 1),
                    np.concatenate([(r < c).astype(np.float32), ones], 1)])
    rhs3 = np.concatenate([rhs, rhs, rhs], axis=1)
    incl = np.stack([(r >= c), (r <= c)]).astype(np.float32)
    strict = np.stack([(r > c), (r < c)]).astype(np.float32)
    levels = []
    s = 1
    while s < C:
        levels.append(((r // (2 * s) == c // (2 * s)) & (r // s != c // s)).astype(np.float32))
        s *= 2
    return (jnp.asarray(tri, BF16), jnp.asarray(rhs3, BF16), jnp.asarray(incl), jnp.asarray(strict),
            jnp.asarray(np.stack(levels)))


def _dn_chunks(ch, consts, with_q):
    C = DN_CHUNK
    tri_ref, rhs3_ref, incl_ref, strict_ref, lev_ref = consts
    n = range(len(ch))
    d = [c["d"] for c in ch]
    last = [C - 1 if di == 0 else 0 for di in d]
    la = [-c["a_scale"] * jax.nn.softplus(c["a_row"] + c["dt_b"]) for c in ch]
    beta = [jax.nn.sigmoid(c["b_row"]) for c in ch]
    dg = []
    for i in n:
        hi = la[i].astype(BF16)
        r1 = la[i] - hi.astype(F32)
        mid = r1.astype(BF16)
        lo = (r1 - mid.astype(F32)).astype(BF16)
        t = tri_ref[d[i]]
        dg.append(_bdot(jnp.concatenate([t * hi, t * mid, t * lo], axis=1), rhs3_ref[d[i]]))
    diff = [g[:, :C] for g in dg]
    gcol = [g[:, C:] for g in dg]
    tail_row = [jnp.exp(diff[i][last[i]:last[i] + 1, :]) for i in n]
    g_last = [jnp.exp(gcol[i][last[i]:last[i] + 1, :]) for i in n]
    dec = [incl_ref[d[i]] * jnp.exp(diff[i] * incl_ref[d[i]]) for i in n]
    egc = [jnp.exp(g) for g in gcol]
    gram = [_bdot(c["k"], c["kT"]) for c in ch]
    nd = [strict_ref[d[i]] * gram[i] * dec[i] * beta[i] for i in n]
    eye = incl_ref[0] * incl_ref[1]
    x = [eye - nd[i] * lev_ref[0] for i in n]
    for lv in range(1, lev_ref.shape[0]):
        xb = [xi.astype(BF16) for xi in x]
        m = [(nd[i] * lev_ref[lv]).astype(BF16) for i in n]
        xm = [_bdot(xb[i], m[i]).astype(BF16) for i in n]
        x = [x[i] - _bdot(xm[i], xb[i]) for i in n]
    ke = [(ch[i]["k"].astype(F32) * egc[i]).astype(BF16) for i in n]
    y = [_bdot(x[i].astype(BF16), jnp.concatenate([ch[i]["v"], ke[i]], axis=1)) for i in n]
    s_b = [c["S"].astype(BF16) for c in ch]
    if with_q:
        qg = [(ch[i]["q"].astype(F32) * egc[i]).astype(BF16) for i in n]
        r2 = [_bdot(jnp.concatenate([y[i][:, DN_DV:].astype(BF16), qg[i]], axis=0), s_b[i]) for i in n]
        yks = [r[:C] for r in r2]
        qs = [r[C:] for r in r2]
    else:
        yks = [_bdot(y[i][:, DN_DV:].astype(BF16), s_b[i]) for i in n]
    vt = [(y[i][:, :DN_DV] - yks[i]).astype(BF16) for i in n]
    kt_tail = [(ch[i]["kT"].astype(F32) * (beta[i] * tail_row[i])).astype(BF16) for i in n]
    s_new = [ch[i]["S"] * g_last[i] + _bdot(kt_tail[i], vt[i]) for i in n]
    if not with_q:
        return s_new, [None for _ in n]
    qk = [_bdot(c["q"], c["kT"]) for c in ch]
    p = [(qk[i] * dec[i] * beta[i]).astype(BF16) for i in n]
    o = [qs[i] + _bdot(p[i], vt[i]) for i in n]
    return s_new, o


def _dn_body(alog_ref, dtb_ref, k_ref, kT_ref, v_ref, q_ref, ab_ref, z_ref, g_ref,
             tri_ref, rhs3_ref, incl_ref, strict_ref, lev_ref, o_ref, acc_ref, s_ref, *, hb, n_lat, n_ctx):
    C = DN_CHUNK
    hg = pl.program_id(1)
    consts = (tri_ref, rhs3_ref, incl_ref, strict_ref, lev_ref)
    acc_ref[...] = jnp.zeros(acc_ref.shape, F32)
    s_ref[...] = jnp.zeros(s_ref.shape, F32)

    def run_chunk(cf, cb, with_q):
        ch, where = [], []
        for h in range(hb):
            head = hg * hb + h
            for d, c in ((0, cf), (1, cb)):
                row0 = pl.multiple_of(c * C, C)
                cols = slice(h * 128, (h + 1) * 128)
                ch.append(dict(
                    k=k_ref[0, pl.ds(row0, C), cols], v=v_ref[0, pl.ds(row0, C), cols], kT=kT_ref[0, h, c],
                    q=q_ref[0, pl.ds(row0, C), cols] if with_q else None,
                    a_row=ab_ref[0, h, d, pl.ds(c, 1), :], b_row=ab_ref[0, h, 2 + d, pl.ds(c, 1), :],
                    a_scale=jnp.exp(alog_ref[d, head]), dt_b=dtb_ref[d, head], S=s_ref[2 * h + d], d=d))
                where.append((row0, cols))
        s_new, o = _dn_chunks(ch, consts, with_q)
        for i, (row0, cols) in enumerate(where):
            s_ref[i] = s_new[i]
            if with_q:
                acc_ref[pl.ds(row0, C), cols] += o[i]

    @pl.loop(0, n_ctx)
    def _(n):
        run_chunk(n_lat + n, n_lat + n_ctx - 1 - n, False)

    @pl.loop(0, n_lat)
    def _(n):
        run_chunk(n, n_lat - 1 - n, True)

    rows = 256 if (n_lat * C) % 256 == 0 else C

    @pl.loop(0, (n_lat * C) // rows)
    def _(i):
        r0 = pl.multiple_of(i * rows, rows)
        for h in range(hb):
            cols = slice(h * 128, (h + 1) * 128)
            od = acc_ref[pl.ds(r0, rows), cols]
            zz = z_ref[0, pl.ds(r0, rows), cols].astype(F32)
            o_ref[0, pl.ds(r0, rows), cols] = (_rms(od, g_ref[...]) * zz).astype(o_ref.dtype)


def _deltanet(k, kT, v, q, ab, z, a_log, dt_bias, g_dn):
    B, S, _ = k.shape
    T = q.shape[1]
    C = DN_CHUNK
    assert C == DN_DK == DN_DV == V7X_LANES
    hb = DN_HEADS_PER_STEP
    n_lat, n_ctx = T // C, (S - T) // C
    consts = list(_dn_consts())
    smem = pl.BlockSpec(memory_space=pltpu.SMEM)
    one = pl.Buffered(1)

    def full(a):
        nd = a.ndim
        return pl.BlockSpec(a.shape, lambda b, g, _n=nd: (0,) * _n)

    return pl.pallas_call(
        functools.partial(_dn_body, hb=hb, n_lat=n_lat, n_ctx=n_ctx),
        out_shape=jax.ShapeDtypeStruct((B, T, DN_V), BF16),
        grid=(B, DN_HEADS // hb),
        in_specs=[smem, smem,
                  pl.BlockSpec((1, S, hb * 128), lambda b, g: (b, 0, g), pipeline_mode=one),
                  pl.BlockSpec((1, hb, S // C, 128, C), lambda b, g: (b, g, 0, 0, 0), pipeline_mode=one),
                  pl.BlockSpec((1, S, hb * 128), lambda b, g: (b, 0, g), pipeline_mode=one),
                  pl.BlockSpec((1, T, hb * 128), lambda b, g: (b, 0, g), pipeline_mode=one),
                  pl.BlockSpec((1, hb, 4, S // C, C), lambda b, g: (b, g, 0, 0, 0)),
                  pl.BlockSpec((1, T, hb * 128), lambda b, g: (b, 0, g), pipeline_mode=one),
                  pl.BlockSpec((1, 128), lambda b, g: (0, 0))] + [full(a) for a in consts],
        out_specs=pl.BlockSpec((1, T, hb * 128), lambda b, g: (b, 0, g)),
        scratch_shapes=[pltpu.VMEM((T, hb * 128), F32), pltpu.VMEM((2 * hb, DN_DK, DN_DV), F32)],
        compiler_params=_cparams(2), name="deltanet",
    )(a_log, dt_bias, k, kT, v, q, ab, z, g_dn, *consts)


def _merge_body(a_ref, b_ref, wa_ref, wb_ref, ga_ref, gb_ref, o_ref):
    oa = _bdot(a_ref[0], wa_ref[...])
    ob = _bdot(b_ref[0], wb_ref[...])
    o_ref[0] = (ga_ref[0].astype(F32) * oa + gb_ref[0].astype(F32) * ob).astype(o_ref.dtype)


def _merge(att, ob, w_oa, w_ob, gates):
    B, T, Ka = att.shape
    D = w_oa.shape[1]
    tm = _pick_tile(T, 1024, 8)
    tn = _pick_tile(D, 512)
    nj = D // tn
    return pl.pallas_call(
        _merge_body,
        out_shape=jax.ShapeDtypeStruct((B, T, D), BF16),
        grid=(B, T // tm, nj),
        in_specs=[pl.BlockSpec((1, tm, Ka), lambda b, i, j: (b, i, 0)),
                  pl.BlockSpec((1, tm, Ka), lambda b, i, j: (b, i, 0)),
                  pl.BlockSpec((Ka, tn), lambda b, i, j: (0, j)),
                  pl.BlockSpec((Ka, tn), lambda b, i, j: (0, j)),
                  pl.BlockSpec((1, tm, tn), lambda b, i, j: (b, i, j)),
                  pl.BlockSpec((1, tm, tn), lambda b, i, j: (b, i, j + nj))],
        out_specs=pl.BlockSpec((1, tm, tn), lambda b, i, j: (b, i, j)),
        compiler_params=_cparams(3), name="merge",
    )(att, ob, w_oa, w_ob, gates, gates)


def _norm_router_body(x_ref, g_ref, sh_ref, sc_ref, wh_ref, wl_ref, br_ref, h_ref, id_ref, gate_ref):
    h = _rms(x_ref[0], g_ref[...]) * (1.0 + sc_ref[0]) + sh_ref[0]
    h_ref[0] = h.astype(h_ref.dtype)
    hh = h.astype(BF16)
    hl = (h - hh.astype(F32)).astype(BF16)
    logits = (_bdot(hh, wh_ref[...]) + _bdot(hl, wh_ref[...]) + _bdot(hh, wl_ref[...])) + br_ref[...]
    lane = lax.broadcasted_iota(jnp.int32, logits.shape, 1)
    neg = jnp.float32(-jnp.inf)
    big = jnp.int32(1 << 20)
    is_g = lane < N_GROUPS
    lg = jnp.where(is_g, logits, neg)
    mg = jnp.max(lg, axis=-1, keepdims=True)
    eg = jnp.where(is_g, jnp.exp(lg - mg), 0.0)
    pg = eg / jnp.sum(eg, axis=-1, keepdims=True)
    pg_top = jnp.max(pg, axis=-1, keepdims=True)
    g_top = jnp.min(jnp.where(jnp.logical_and(is_g, pg == pg_top), lane, big), axis=-1, keepdims=True)
    lo = N_GROUPS + EXPERTS_PER_GROUP * g_top
    is_e = jnp.logical_and(lane >= lo, lane < lo + EXPERTS_PER_GROUP)
    le = jnp.where(is_e, logits, neg)
    me = jnp.max(le, axis=-1, keepdims=True)
    ee = jnp.where(is_e, jnp.exp(le - me), 0.0)
    pe = ee / jnp.sum(ee, axis=-1, keepdims=True)
    p1 = jnp.max(pe, axis=-1, keepdims=True)
    i1 = jnp.min(jnp.where(jnp.logical_and(is_e, pe == p1), lane, big), axis=-1, keepdims=True)
    rest = jnp.logical_and(is_e, lane != i1)
    pe2 = jnp.where(rest, pe, -1.0)
    p2 = jnp.max(pe2, axis=-1, keepdims=True)
    i2 = jnp.min(jnp.where(jnp.logical_and(rest, pe2 == p2), lane, big), axis=-1, keepdims=True)
    den = p1 + p2
    id_ref[0] = jnp.where(lane == 0, i1 - N_GROUPS, jnp.where(lane == 1, i2 - N_GROUPS, 0))
    gate_ref[0] = jnp.where(lane == 0, p1 / den * pg_top, jnp.where(lane == 1, p2 / den * pg_top, 0.0))


def _norm_router(x, g, shift, scale, w_hi, w_lo, b_r):
    B, T, D = x.shape
    tr = _pick_tile(T, 256, 8)
    return pl.pallas_call(
        _norm_router_body,
        out_shape=(jax.ShapeDtypeStruct((B, T, D), F32),
                   jax.ShapeDtypeStruct((B, T, 128), jnp.int32),
                   jax.ShapeDtypeStruct((B, T, 128), F32)),
        grid=(B, T // tr),
        in_specs=[pl.BlockSpec((1, tr, D), lambda b, i: (b, i, 0)),
                  pl.BlockSpec((1, D), lambda b, i: (0, 0)),
                  pl.BlockSpec((1, 1, D), lambda b, i: (b, 0, 0)),
                  pl.BlockSpec((1, 1, D), lambda b, i: (b, 0, 0)),
                  pl.BlockSpec((D, 128), lambda b, i: (0, 0)),
                  pl.BlockSpec((D, 128), lambda b, i: (0, 0)),
                  pl.BlockSpec((1, 128), lambda b, i: (0, 0))],
        out_specs=(pl.BlockSpec((1, tr, D), lambda b, i: (b, i, 0)),
                   pl.BlockSpec((1, tr, 128), lambda b, i: (b, i, 0)),
                   pl.BlockSpec((1, tr, 128), lambda b, i: (b, i, 0))),
        compiler_params=_cparams(2), name="norm_router",
    )(x, g, shift, scale, w_hi, w_lo, b_r)


def _row_gather(idx_ref, src_ref, buf_ref, sem, slot, n_rows):
    def row_copy(r, s):
        return pltpu.make_async_copy(src_ref.at[pl.ds(s, 1)], buf_ref.at[slot, pl.ds(r, 1)], sem.at[slot])

    def start():
        def body(r, c):
            row_copy(r, idx_ref[0, 0, r]).start()
            return c
        lax.fori_loop(0, n_rows, body, 0, unroll=8)

    def wait():
        def body(r, c):
            row_copy(r, 0).wait()
            return c
        lax.fori_loop(0, n_rows, body, 0, unroll=8)

    return start, wait


def _moe_gather_body(nu_ref, src_ref, nxt_ref, h_ref, o_ref, buf_ref, sem):
    i = pl.program_id(0)
    slot = lax.rem(i, 2)
    start_cur, wait_cur = _row_gather(src_ref, h_ref, buf_ref, sem, slot, MOE_ROWS)
    start_nxt, _ = _row_gather(nxt_ref, h_ref, buf_ref, sem, 1 - slot, MOE_ROWS)

    @pl.when(jnp.logical_and(i == 0, nu_ref[0] > 0))
    def _():
        start_cur()

    @pl.when(i + 1 < nu_ref[0])
    def _():
        start_nxt()

    @pl.when(i < nu_ref[0])
    def _():
        wait_cur()
        o_ref[...] = buf_ref[slot].astype(o_ref.dtype)

    @pl.when(i >= nu_ref[0])
    def _():
        o_ref[...] = jnp.zeros(o_ref.shape, o_ref.dtype)


def _moe_gather(h, src, n_used):
    n, D = h.shape
    R = src.shape[0]
    nb = R // MOE_ROWS
    src3 = src.reshape(nb, 1, MOE_ROWS)
    return pl.pallas_call(
        _moe_gather_body,
        out_shape=jax.ShapeDtypeStruct((R, D), BF16),
        grid_spec=pltpu.PrefetchScalarGridSpec(
            num_scalar_prefetch=1, grid=(nb,),
            in_specs=[pl.BlockSpec((1, 1, MOE_ROWS), lambda i, nu: (i, 0, 0), memory_space=pltpu.SMEM),
                      pl.BlockSpec((1, 1, MOE_ROWS), lambda i, nu: (jnp.minimum(i + 1, nb - 1), 0, 0),
                                   memory_space=pltpu.SMEM),
                      pl.BlockSpec(memory_space=pl.ANY)],
            out_specs=pl.BlockSpec((MOE_ROWS, D), lambda i, nu: (i, 0)),
            scratch_shapes=[pltpu.VMEM((2, MOE_ROWS, D), F32), pltpu.SemaphoreType.DMA((2,))]),
        compiler_params=_cparams(1), name="moe_gather",
    )(n_used, src3, src3, h)


def _moe_up_body(be_ref, nu_ref, x_ref, w1_ref, w3_ref, o_ref, w1b_ref, w3b_ref):
    i = pl.program_id(1)
    changed = jnp.logical_or(i == 0, be_ref[i] != be_ref[jnp.maximum(i - 1, 0)])

    @pl.when(changed)
    def _():
        w1b_ref[...] = w1_ref[0].astype(BF16)
        w3b_ref[...] = w3_ref[0].astype(BF16)

    @pl.when(i < nu_ref[0])
    def _():
        x = x_ref[...]
        a = _bdot(x, w1b_ref[...])
        b = _bdot(x, w3b_ref[...])
        o_ref[...] = (_silu(a) * b).astype(o_ref.dtype)

    @pl.when(i >= nu_ref[0])
    def _():
        o_ref[...] = jnp.zeros(o_ref.shape, o_ref.dtype)


def _moe_down_body(be_ref, nu_ref, h_ref, w2_ref, o_ref, w2b_ref):
    i = pl.program_id(1)
    changed = jnp.logical_or(i == 0, be_ref[i] != be_ref[jnp.maximum(i - 1, 0)])

    @pl.when(changed)
    def _():
        w2b_ref[...] = w2_ref[0].astype(BF16)

    @pl.when(i < nu_ref[0])
    def _():
        o_ref[...] = _bdot(h_ref[...], w2b_ref[...]).astype(o_ref.dtype)

    @pl.when(i >= nu_ref[0])
    def _():
        o_ref[...] = jnp.zeros(o_ref.shape, o_ref.dtype)


def _moe_experts(x_pad, blk_expert, n_used, w1, w3, w2):
    R, D = x_pad.shape
    nb = R // MOE_ROWS
    Hd = w1.shape[2]
    th = _pick_tile(Hd, 512)
    hbuf = pl.pallas_call(
        _moe_up_body,
        out_shape=jax.ShapeDtypeStruct((R, Hd), BF16),
        grid_spec=pltpu.PrefetchScalarGridSpec(
            num_scalar_prefetch=2, grid=(Hd // th, nb),
            in_specs=[pl.BlockSpec((MOE_ROWS, D), lambda j, i, be, nu: (i, 0)),
                      pl.BlockSpec((1, D, th), lambda j, i, be, nu: (be[i], 0, j)),
                      pl.BlockSpec((1, D, th), lambda j, i, be, nu: (be[i], 0, j))],
            out_specs=pl.BlockSpec((MOE_ROWS, th), lambda j, i, be, nu: (i, j)),
            scratch_shapes=[pltpu.VMEM((D, th), BF16), pltpu.VMEM((D, th), BF16)]),
        compiler_params=_cparams(2), name="moe_up",
    )(blk_expert, n_used, x_pad, w1, w3)
    tn = _pick_tile(D, 2048)
    return pl.pallas_call(
        _moe_down_body,
        out_shape=jax.ShapeDtypeStruct((R, D), F32),
        grid_spec=pltpu.PrefetchScalarGridSpec(
            num_scalar_prefetch=2, grid=(D // tn, nb),
            in_specs=[pl.BlockSpec((MOE_ROWS, Hd), lambda j, i, be, nu: (i, 0)),
                      pl.BlockSpec((1, Hd, tn), lambda j, i, be, nu: (be[i], 0, j))],
            out_specs=pl.BlockSpec((MOE_ROWS, tn), lambda j, i, be, nu: (i, j)),
            scratch_shapes=[pltpu.VMEM((Hd, tn), BF16)]),
        compiler_params=_cparams(2), name="moe_down",
    )(blk_expert, n_used, hbuf, w2)


def _combine_body(dest_ref, nxt_ref, x_ref, p_ref, g_ref, y_ref, o_ref, buf_ref, sem, *, tr, n_steps):
    t = pl.program_id(0)
    slot = lax.rem(t, 2)
    start_cur, wait_cur = _row_gather(dest_ref, y_ref, buf_ref, sem, slot, TOP_K * tr)
    start_nxt, _ = _row_gather(nxt_ref, y_ref, buf_ref, sem, 1 - slot, TOP_K * tr)

    @pl.when(t == 0)
    def _():
        start_cur()

    @pl.when(t + 1 < n_steps)
    def _():
        start_nxt()

    wait_cur()
    p = p_ref[0]
    ffn = p[:, 0:1] * buf_ref[slot, 0:tr, :] + p[:, 1:2] * buf_ref[slot, tr:2 * tr, :]
    o_ref[0] = x_ref[0] + g_ref[0] * ffn


def _combine(x, y_pad, dest, gate, g2):
    B, T, D = x.shape
    tr = _pick_tile(T, 256, 8)
    nt = T // tr
    n_steps = B * nt
    dest_t = dest.reshape(n_steps, tr, TOP_K).transpose(0, 2, 1).reshape(n_steps, 1, TOP_K * tr)
    row = pl.BlockSpec((1, tr, D), lambda t: (t // nt, t % nt, 0))
    return pl.pallas_call(
        functools.partial(_combine_body, tr=tr, n_steps=n_steps),
        out_shape=jax.ShapeDtypeStruct((B, T, D), F32),
        grid=(n_steps,),
        in_specs=[pl.BlockSpec((1, 1, TOP_K * tr), lambda t: (t, 0, 0), memory_space=pltpu.SMEM),
                  pl.BlockSpec((1, 1, TOP_K * tr), lambda t: (jnp.minimum(t + 1, n_steps - 1), 0, 0),
                               memory_space=pltpu.SMEM),
                  row,
                  pl.BlockSpec((1, tr, 128), lambda t: (t // nt, t % nt, 0)),
                  pl.BlockSpec((1, 1, D), lambda t: (t // nt, 0, 0)),
                  pl.BlockSpec(memory_space=pl.ANY)],
        out_specs=row,
        scratch_shapes=[pltpu.VMEM((2, TOP_K * tr, D), F32), pltpu.SemaphoreType.DMA((2,))],
        compiler_params=_cparams(1), name="moe_combine",
    )(dest_t, dest_t, x, gate, g2, y_pad)


def _rope_tables(T, Tc):
    rows = T // GRID_W
    row = jnp.repeat(jnp.arange(rows), GRID_W).astype(F32)
    col = jnp.tile(jnp.arange(GRID_W), rows).astype(F32)
    half = QK_ROPE // 2
    inv_freq = ROPE_THETA ** (-jnp.arange(0, half, 2, dtype=F32) / half)
    ang = jnp.concatenate([row[:, None] * inv_freq, col[:, None] * inv_freq], axis=-1)
    cos, sin = jnp.cos(ang), jnp.sin(ang)
    z64 = jnp.zeros((T, 64), F32)
    lat = jnp.concatenate([cos, cos, z64, -sin, sin, z64], axis=-1)
    ctx = jnp.concatenate([jnp.ones((Tc, 64), F32), jnp.zeros((Tc, 192), F32)], axis=-1)
    return jnp.concatenate([lat, ctx], axis=0)


_PERM = np.concatenate([np.arange(0, QK_ROPE, 2), np.arange(1, QK_ROPE, 2)])
_PERM_SW = np.concatenate([np.arange(1, QK_ROPE, 2), np.arange(0, QK_ROPE, 2)])


def _pad_cols(a, n):
    return jnp.pad(a, ((0, 0), (0, n - a.shape[1])))


def _moe_dispatch(expert_ids, n_tokens):
    a = n_tokens * TOP_K
    e_flat = expert_ids.reshape(a)
    onehot = (e_flat[:, None] == jnp.arange(N_EXPERTS)[None, :]).astype(F32)
    ch = 128 if a % 128 == 0 else a
    nc = a // ch
    hp = lax.Precision.HIGHEST
    within = jnp.einsum("ij,cje->cie", jnp.tril(jnp.ones((ch, ch), F32)), onehot.reshape(nc, ch, N_EXPERTS),
                        precision=hp)
    before = jnp.einsum("ij,je->ie", jnp.tril(jnp.ones((nc, nc), F32), -1), within[:, -1, :], precision=hp)
    csum = (within + before[:, None, :]).reshape(a, N_EXPERTS)
    rank = jnp.sum(csum * onehot, axis=1).astype(jnp.int32) - 1
    counts = csum[-1].astype(jnp.int32)
    padded = (counts + MOE_ROWS - 1) // MOE_ROWS * MOE_ROWS
    pad_end = jnp.cumsum(padded)
    pad_start = pad_end - padded
    dest = pad_start[e_flat] + rank
    n_blocks = -(-a // MOE_ROWS) + N_EXPERTS
    src = jnp.zeros((n_blocks * MOE_ROWS,), jnp.int32).at[dest].set(jnp.arange(a, dtype=jnp.int32) // TOP_K)
    blk_expert = jnp.minimum(
        jnp.searchsorted(pad_end, jnp.arange(n_blocks) * MOE_ROWS, side="right"), N_EXPERTS - 1).astype(jnp.int32)
    n_used = (pad_end[-1] // MOE_ROWS).astype(jnp.int32).reshape(1)
    return dest, src, blk_expert, n_used


def _layer(x, c, ctx, c_ctx, p):
    B, T, D = x.shape
    Tc = ctx.shape[1]
    S = T + Tc
    H = MLA_HEADS

    cin = jnp.zeros((1, 16, D), F32).at[0, :B].set(c).at[0, B].set(c_ctx)
    mod = _matmul(cin, p["w_mod"], tm=16, tn=1024, out_dtype=F32, pre=_silu,
                  epi=lambda acc, b: acc + b, extras=[(p["b_mod"].reshape(1, 1, -1), "bcol")], name="mod")[0]
    mod_l = mod[:B].reshape(B, N_MOD, 1, D)
    mod_c = mod[B].reshape(N_MOD, 1, 1, D)

    h = _norm_mod(x, ctx, p["norm1_g"].reshape(1, D), mod_l[:, 0], mod_l[:, 1], mod_c[0], mod_c[1])

    w_in = p["w_in"]
    wkr = w_in[:, OFF_KR:OFF_AB]
    w_small = jnp.concatenate([w_in[:, OFF_CKV:OFF_KR], _pad_cols(wkr[:, _PERM], 128),
                               _pad_cols(wkr[:, _PERM_SW], 128), _pad_cols(w_in[:, OFF_AB:OFF_DK], 128)],
                              axis=1).astype(BF16)
    p_small = _matmul(h, w_small, tm=1152, tn=896, out_dtype=F32, name="in_small")
    p_dkv = _matmul(h, w_in[:, OFF_DK:OFF_DQ].astype(BF16), tm=1152, name="in_dkv")
    p_dq = _matmul(h, w_in[:, OFF_DQ:OFF_CQ].astype(BF16), rows=T, name="in_dq")
    p_cq = _matmul(h, w_in[:, OFF_CQ:OFF_Z].astype(BF16), rows=T, out_dtype=F32, name="in_cq")
    z_act = _matmul(h, w_in[:, OFF_Z:OFF_GATE].astype(BF16), rows=T, epi=_silu, name="in_z")
    gates = _matmul(h, w_in[:, OFF_GATE:].astype(BF16), rows=T, epi=jax.nn.sigmoid, name="in_gate")

    tab = _rope_tables(T, Tc)
    gkr = p["k_rope_norm_g"]
    ckv_n, k_rope = _mla_small(p_small, tab, p["kv_norm_g"].reshape(1, -1),
                               _pad_cols(gkr[_PERM][None], 128), _pad_cols(gkr[_PERM_SW][None], 128))
    w_ukv = p["w_ukv"].reshape(KV_LORA, H, QK_NOPE + V_HEAD)
    hg = HEAD_GROUP
    w_kv_perm = jnp.concatenate(
        [w_ukv[:, :, :QK_NOPE].reshape(KV_LORA, H // hg, hg * QK_NOPE),
         w_ukv[:, :, QK_NOPE:].reshape(KV_LORA, H // hg, hg * V_HEAD)], axis=2).reshape(KV_LORA, H * 256).astype(BF16)
    k_all, v_all = _kv_up(ckv_n, w_kv_perm, k_rope, p["k_norm_g"].reshape(1, -1))

    w_uq = p["w_uq"].reshape(Q_LORA, H, QK_NOPE + QK_ROPE)
    wq_r = w_uq[:, :, QK_NOPE:]
    zpad = jnp.zeros((Q_LORA, H, 64), F32)
    w_q_perm = jnp.concatenate(
        [w_uq[:, :, :QK_NOPE].reshape(Q_LORA, H // hg, hg * 128),
         jnp.concatenate([wq_r[:, :, _PERM], zpad], axis=2).reshape(Q_LORA, H // hg, hg * 128),
         jnp.concatenate([wq_r[:, :, _PERM_SW], zpad], axis=2).reshape(Q_LORA, H // hg, hg * 128)],
        axis=2).reshape(Q_LORA, H * 384).astype(BF16)
    gqr = p["q_rope_norm_g"]
    q_all = _q_up(p_cq, w_q_perm, tab[:T], p["q_norm_g"].reshape(1, -1), p["q_a_norm_g"].reshape(1, -1),
                  _pad_cols(gqr[_PERM][None], 128), _pad_cols(gqr[_PERM_SW][None], 128))
    att = _attention(q_all, k_all, v_all)

    cw = jnp.pad(p["conv_w"], ((0, 8 - CONV_W), (0, 0)))
    dn_k = _short_conv(p_dkv, 0, DN_QK, cw[:, :DN_QK], n_lat_rows=T, l2=True)
    dn_v = _short_conv(p_dkv, DN_QK, DN_V, cw[:, DN_QK:DN_QK + DN_V], n_lat_rows=T, l2=False)
    dn_q = _short_conv(p_dq, 0, DN_QK, cw[:, DN_QK + DN_V:], n_lat_rows=T, l2=True, scale=float(DN_DK ** -0.5))
    C = DN_CHUNK
    dn_kT = dn_k.reshape(B, S // C, C, DN_HEADS, DN_DK).transpose(0, 3, 1, 4, 2)
    ab = p_small[:, :, 768:768 + 4 * DN_HEADS].reshape(B, S // C, C, 4, DN_HEADS).transpose(0, 4, 3, 1, 2)
    ob_in = _deltanet(dn_k, dn_kT, dn_v, dn_q, ab, z_act, p["a_log"], p["dt_bias"], p["dn_norm_g"].reshape(1, -1))

    mix = _merge(att, ob_in, p["w_oa"].astype(BF16), p["w_ob"].astype(BF16), gates)
    xl = _matmul(mix, p["w_out"].astype(BF16), out_dtype=F32,
                 epi=lambda acc, xr, g: xr + g * acc, extras=[(x, "tile"), (mod_l[:, 2], "bcol")], name="out_proj")

    w_r = _pad_cols(jnp.concatenate([p["w_rg"], p["w_re"]], axis=1), 128)
    w_r_hi = w_r.astype(BF16)
    w_r_lo = (w_r - w_r_hi.astype(F32)).astype(BF16)
    b_r = _pad_cols(jnp.concatenate([p["b_rg"], p["b_re"]])[None], 128)
    h2, ids, gate = _norm_router(xl, p["norm2_g"].reshape(1, D), mod_l[:, 3], mod_l[:, 4], w_r_hi, w_r_lo, b_r)
    n = B * T
    dest, src, blk_expert, n_used = _moe_dispatch(ids[:, :, :TOP_K].reshape(n, TOP_K), n)
    x_pad = _moe_gather(h2.reshape(n, D), src, n_used)
    y_pad = _moe_experts(x_pad, blk_expert, n_used, p["w1"], p["w3"], p["w2"])
    return _combine(xl, y_pad, dest.reshape(n, TOP_K), gate, mod_l[:, 5])


def kernel(x, c, ctx, c_ctx, norm1_g, norm2_g, w_mod, b_mod, w_in, q_a_norm_g, w_uq, kv_norm_g, w_ukv, q_norm_g,
           q_rope_norm_g, k_norm_g, k_rope_norm_g, conv_w, a_log, dt_bias, dn_norm_g, w_oa, w_ob, w_out, w_rg, b_rg,
           w_re, b_re, w1, w3, w2):
    depth = norm1_g.shape[0]
    assert depth == 1, "single-layer block: the context stream is read, never updated"
    layer = 0
    p = {
        "norm1_g": norm1_g[layer], "norm2_g": norm2_g[layer], "w_mod": w_mod[layer], "b_mod": b_mod[layer],
        "w_in": w_in[layer], "q_a_norm_g": q_a_norm_g[layer], "w_uq": w_uq[layer], "kv_norm_g": kv_norm_g[layer],
        "w_ukv": w_ukv[layer], "q_norm_g": q_norm_g[layer], "q_rope_norm_g": q_rope_norm_g[layer],
        "k_norm_g": k_norm_g[layer], "k_rope_norm_g": k_rope_norm_g[layer], "conv_w": conv_w[layer],
        "a_log": a_log[layer], "dt_bias": dt_bias[layer], "dn_norm_g": dn_norm_g[layer], "w_oa": w_oa[layer],
        "w_ob": w_ob[layer], "w_out": w_out[layer], "w_rg": w_rg[layer], "b_rg": b_rg[layer], "w_re": w_re[layer],
        "b_re": b_re[layer], "w1": w1[layer], "w3": w3[layer], "w2": w2[layer],
    }
    return _layer(x, c, ctx, c_ctx, p)
```

```python
import functools

import numpy as np
import jax
import jax.numpy as jnp
from jax import lax
from jax.experimental import pallas as pl
from jax.experimental.pallas import tpu as pltpu

F32 = jnp.float32
BF16 = jnp.bfloat16

EPS = 1e-6
GRID_W = 64
N_MOD = 6
MLA_HEADS = 16
Q_LORA = 1024
KV_LORA = 512
QK_NOPE = 128
QK_ROPE = 64
V_HEAD = 128
ROPE_THETA = 10000.0
DN_HEADS = 16
DN_DK = 128
DN_DV = 128
DN_QK = DN_HEADS * DN_DK
DN_V = DN_HEADS * DN_DV
CONV_W = 5
N_GROUPS = 4
EXPERTS_PER_GROUP = 8
N_EXPERTS = N_GROUPS * EXPERTS_PER_GROUP
TOP_K = 2
EXPERT_HIDDEN = 1024

OFF_CKV = 0
OFF_KR = OFF_CKV + KV_LORA
OFF_AB = OFF_KR + QK_ROPE
OFF_DK = OFF_AB + 4 * DN_HEADS
OFF_DV = OFF_DK + DN_QK
OFF_DQ = OFF_DV + DN_V
OFF_CQ = OFF_DQ + DN_QK
OFF_Z = OFF_CQ + Q_LORA
OFF_GATE = OFF_Z + DN_V

V7X_LANES = 128
V7X_VMEM_BYTES = 64 * 1024 * 1024
VMEM_LIMIT = V7X_VMEM_BYTES - 8 * 1024 * 1024

DN_CHUNK = 128
MOE_ROWS = 512
MOE_DOWN_COLS = 2048
HEAD_GROUP = 4
DN_HEADS_PER_STEP = 8
ATTN_ROWS = 1024
ATTN_SPLIT = 4
CONV_HALO = 16


def _cparams(n_grid):
    return pltpu.CompilerParams(dimension_semantics=("arbitrary",) * n_grid, vmem_limit_bytes=VMEM_LIMIT)


def _pick_tile(n, target, mult=V7X_LANES):
    if n <= target:
        return n
    best = None
    t = mult
    while t <= target:
        if n % t == 0:
            best = t
        t += mult
    assert best is not None, (n, target)
    return best


def _rms(x, g):
    return x * lax.rsqrt(jnp.mean(x * x, axis=-1, keepdims=True) + EPS) * g


def _silu(x):
    return x * jax.nn.sigmoid(x)


def _bdot(a, b):
    return jnp.dot(a, b, preferred_element_type=F32)


def _pack_bf16_pair(lo, hi):
    lo_bits = lax.bitcast_convert_type(lo.astype(BF16).astype(F32), jnp.uint32)
    hi_bits = lax.bitcast_convert_type(hi.astype(BF16).astype(F32), jnp.uint32)
    return (lo_bits >> 16) | (hi_bits & jnp.uint32(0xFFFF0000))


def _unpack_bf16_pair(w):
    lo = lax.bitcast_convert_type(w << 16, F32)
    hi = lax.bitcast_convert_type(w & jnp.uint32(0xFFFF0000), F32)
    return lo, hi


def _mm_body(*refs, pre, epi, n_extra, cached):
    x_ref, w_ref = refs[0], refs[1]
    extra = refs[2:2 + n_extra]
    o_ref = refs[2 + n_extra]
    if cached:
        xs_ref = refs[3 + n_extra]

        @pl.when(pl.program_id(2) == 0)
        def _():
            xs_ref[...] = pre(x_ref[0]).astype(BF16)

        x = xs_ref[...]
    else:
        x = x_ref[0]
    w = w_ref[...]
    if w.dtype != BF16:
        w = w.astype(BF16)
    acc = _bdot(x, w)
    if epi is not None:
        acc = epi(acc, *[e[0] for e in extra])
    o_ref[0] = acc.astype(o_ref.dtype)


def _matmul(x, w, *, rows=None, tm=1024, tn=512, out_dtype=BF16, pre=None, epi=None, extras=(), name="mm"):
    B, S, K = x.shape
    N = w.shape[1]
    rows = S if rows is None else rows
    tm = _pick_tile(rows, tm)
    tn = _pick_tile(N, tn)
    grid = (B, rows // tm, N // tn)
    in_specs = [pl.BlockSpec((1, tm, K), lambda b, i, j: (b, i, 0)),
                pl.BlockSpec((K, tn), lambda b, i, j: (0, j))]
    args = [x, w]
    for arr, kind in extras:
        if kind == "tile":
            in_specs.append(pl.BlockSpec((1, tm, tn), lambda b, i, j: (b, i, j)))
        elif kind == "bcol":
            if arr.shape[0] == 1:
                in_specs.append(pl.BlockSpec((1, 1, tn), lambda b, i, j: (0, 0, j)))
            else:
                in_specs.append(pl.BlockSpec((1, 1, tn), lambda b, i, j: (b, 0, j)))
        else:
            raise ValueError(kind)
        args.append(arr)
    cached = pre is not None
    scratch = [pltpu.VMEM((tm, K), BF16)] if cached else []
    if not cached:
        assert x.dtype == BF16
    return pl.pallas_call(
        functools.partial(_mm_body, pre=pre, epi=epi, n_extra=len(extras), cached=cached),
        out_shape=jax.ShapeDtypeStruct((B, rows, N), out_dtype),
        grid=grid, in_specs=in_specs,
        out_specs=pl.BlockSpec((1, tm, tn), lambda b, i, j: (b, i, j)),
        scratch_shapes=scratch, compiler_params=_cparams(3), name=name,
    )(*args)


def _norm_mod_body(x_ref, c_ref, g_ref, sl_ref, scl_ref, sc_ref, scc_ref, o_ref, *, n_lat):
    j = pl.program_id(1)

    @pl.when(j < n_lat)
    def _():
        h = _rms(x_ref[0], g_ref[...])
        o_ref[0] = (h * (1.0 + scl_ref[0]) + sl_ref[0]).astype(o_ref.dtype)

    @pl.when(j >= n_lat)
    def _():
        h = _rms(c_ref[0], g_ref[...])
        o_ref[0] = (h * (1.0 + scc_ref[0]) + sc_ref[0]).astype(o_ref.dtype)


def _norm_mod(x, ctx, g, shift_l, scale_l, shift_c, scale_c):
    B, T, D = x.shape
    Tc = ctx.shape[1]
    tr = _pick_tile(int(np.gcd(T, Tc)), 256, 8)
    n_lat, n_ctx = T // tr, Tc // tr
    return pl.pallas_call(
        functools.partial(_norm_mod_body, n_lat=n_lat),
        out_shape=jax.ShapeDtypeStruct((B, T + Tc, D), BF16),
        grid=(B, n_lat + n_ctx),
        in_specs=[pl.BlockSpec((1, tr, D), lambda b, j: (b, jnp.minimum(j, n_lat - 1), 0)),
                  pl.BlockSpec((1, tr, D), lambda b, j: (b, jnp.maximum(j - n_lat, 0), 0)),
                  pl.BlockSpec((1, D), lambda b, j: (0, 0)),
                  pl.BlockSpec((1, 1, D), lambda b, j: (b, 0, 0)),
                  pl.BlockSpec((1, 1, D), lambda b, j: (b, 0, 0)),
                  pl.BlockSpec((1, 1, D), lambda b, j: (0, 0, 0)),
                  pl.BlockSpec((1, 1, D), lambda b, j: (0, 0, 0))],
        out_specs=pl.BlockSpec((1, tr, D), lambda b, j: (b, j, 0)),
        compiler_params=_cparams(2), name="norm_mod",
    )(x, ctx, g, shift_l, scale_l, shift_c, scale_c)


def _mla_small_body(p_ref, tab_ref, gkv_ref, gkr_ref, gks_ref, ckv_ref, kr_ref):
    p = p_ref[0]
    ckv_ref[0] = _rms(p[:, :KV_LORA], gkv_ref[...]).astype(ckv_ref.dtype)
    xr = p[:, KV_LORA:KV_LORA + 128]
    xs = p[:, KV_LORA + 128:KV_LORA + 256]
    inv = lax.rsqrt(jnp.sum(xr * xr, axis=-1, keepdims=True) * (1.0 / QK_ROPE) + EPS)
    tab = tab_ref[...]
    kr = inv * (xr * gkr_ref[...] * tab[:, :128] + xs * gks_ref[...] * tab[:, 128:])
    kr_ref[0] = kr.astype(kr_ref.dtype)


def _mla_small(p_small, tab, g_kv, g_kr, g_ks):
    B, S, W = p_small.shape
    tr = _pick_tile(S, 768, 8)
    return pl.pallas_call(
        _mla_small_body,
        out_shape=(jax.ShapeDtypeStruct((B, S, KV_LORA), BF16), jax.ShapeDtypeStruct((B, S, 128), BF16)),
        grid=(B, S // tr),
        in_specs=[pl.BlockSpec((1, tr, 768), lambda b, i: (b, i, 0)),
                  pl.BlockSpec((tr, 256), lambda b, i: (i, 0)),
                  pl.BlockSpec((1, KV_LORA), lambda b, i: (0, 0)),
                  pl.BlockSpec((1, 128), lambda b, i: (0, 0)),
                  pl.BlockSpec((1, 128), lambda b, i: (0, 0))],
        out_specs=(pl.BlockSpec((1, tr, KV_LORA), lambda b, i: (b, i, 0)),
                   pl.BlockSpec((1, tr, 128), lambda b, i: (b, i, 0))),
        compiler_params=_cparams(2), name="mla_small",
    )(p_small, tab, g_kv, g_kr, g_ks)


def _kv_up_body(x_ref, w_ref, kr_ref, gk_ref, k_ref, v_ref, *, hg):
    acc = _bdot(x_ref[0], w_ref[...])
    kr = kr_ref[0]
    for h in range(hg):
        kn = _rms(acc[:, h * 128:(h + 1) * 128], gk_ref[...])
        k_ref[0, :, h * 256:h * 256 + 128] = kn.astype(k_ref.dtype)
        k_ref[0, :, h * 256 + 128:(h + 1) * 256] = kr
    v_ref[0] = acc[:, hg * 128:].astype(v_ref.dtype)


def _kv_up(ckv, w_perm, kr, g_k):
    B, S, _ = ckv.shape
    hg = HEAD_GROUP
    tm = _pick_tile(S, 1152, 8)
    return pl.pallas_call(
        functools.partial(_kv_up_body, hg=hg),
        out_shape=(jax.ShapeDtypeStruct((B, S, MLA_HEADS * 256), BF16),
                   jax.ShapeDtypeStruct((B, S, MLA_HEADS * V_HEAD), BF16)),
        grid=(B, S // tm, MLA_HEADS // hg),
        in_specs=[pl.BlockSpec((1, tm, KV_LORA), lambda b, i, j: (b, i, 0)),
                  pl.BlockSpec((KV_LORA, hg * 256), lambda b, i, j: (0, j)),
                  pl.BlockSpec((1, tm, 128), lambda b, i, j: (b, i, 0)),
                  pl.BlockSpec((1, 128), lambda b, i, j: (0, 0))],
        out_specs=(pl.BlockSpec((1, tm, hg * 256), lambda b, i, j: (b, i, j)),
                   pl.BlockSpec((1, tm, hg * 128), lambda b, i, j: (b, i, j))),
        compiler_params=_cparams(3), name="kv_up",
    )(ckv, w_perm, kr, g_k)


def _q_up_body(x_ref, w_ref, tab_ref, gq_ref, gqa_ref, gr_ref, gs_ref, q_ref, xs_ref, *, hg, scale):
    @pl.when(pl.program_id(2) == 0)
    def _():
        xs_ref[...] = _rms(x_ref[0], gqa_ref[...]).astype(BF16)

    acc = _bdot(xs_ref[...], w_ref[...])
    tab = tab_ref[...]
    for h in range(hg):
        qn = _rms(acc[:, h * 128:(h + 1) * 128], gq_ref[...]) * scale
        xr = acc[:, (hg + h) * 128:(hg + h + 1) * 128]
        xw = acc[:, (2 * hg + h) * 128:(2 * hg + h + 1) * 128]
        inv = lax.rsqrt(jnp.sum(xr * xr, axis=-1, keepdims=True) * (1.0 / QK_ROPE) + EPS) * scale
        qr = inv * (xr * gr_ref[...] * tab[:, :128] + xw * gs_ref[...] * tab[:, 128:])
        q_ref[0, :, h * 256:h * 256 + 128] = qn.astype(q_ref.dtype)
        q_ref[0, :, h * 256 + 128:(h + 1) * 256] = qr.astype(q_ref.dtype)


def _q_up(cq, w_perm, tab, g_q, g_qa, g_r, g_s):
    B, T, _ = cq.shape
    hg = HEAD_GROUP
    tm = _pick_tile(T, 512, 8)
    scale = float((QK_NOPE + QK_ROPE) ** -0.5 * np.log2(np.e))
    return pl.pallas_call(
        functools.partial(_q_up_body, hg=hg, scale=scale),
        out_shape=jax.ShapeDtypeStruct((B, T, MLA_HEADS * 256), BF16),
        grid=(B, T // tm, MLA_HEADS // hg),
        in_specs=[pl.BlockSpec((1, tm, Q_LORA), lambda b, i, j: (b, i, 0)),
                  pl.BlockSpec((Q_LORA, hg * 384), lambda b, i, j: (0, j)),
                  pl.BlockSpec((tm, 256), lambda b, i, j: (i, 0)),
                  pl.BlockSpec((1, 128), lambda b, i, j: (0, 0)),
                  pl.BlockSpec((1, Q_LORA), lambda b, i, j: (0, 0)),
                  pl.BlockSpec((1, 128), lambda b, i, j: (0, 0)),
                  pl.BlockSpec((1, 128), lambda b, i, j: (0, 0))],
        out_specs=pl.BlockSpec((1, tm, hg * 256), lambda b, i, j: (b, i, j)),
        scratch_shapes=[pltpu.VMEM((tm, Q_LORA), BF16)],
        compiler_params=_cparams(3), name="q_up",
    )(cq, w_perm, tab, g_q, g_qa, g_r, g_s)


def _attn_body(q_ref, k_ref, v_ref, o_ref, *, n_split):
    k, v = k_ref[0], v_ref[0]
    rows = q_ref.shape[1] // n_split
    sl = [slice(i * rows, (i + 1) * rows) for i in range(n_split)]
    s = [lax.dot_general(q_ref[0, r, :], k, (((1,), (1,)), ((), ())), preferred_element_type=F32) for r in sl]
    m = [jnp.max(x, axis=-1, keepdims=True) for x in s]
    p = [jnp.exp2(x - mx) for x, mx in zip(s, m)]
    l = [jnp.sum(x, axis=-1, keepdims=True) for x in p]
    o = [_bdot(x.astype(BF16), v) for x in p]
    for r, ox, lx in zip(sl, o, l):
        o_ref[0, r, :] = (ox / lx).astype(o_ref.dtype)


def _attention(q, k, v):
    B, T, _ = q.shape
    S = k.shape[1]
    tq = _pick_tile(T, ATTN_ROWS, 8)
    return pl.pallas_call(
        functools.partial(_attn_body, n_split=ATTN_SPLIT if tq % (8 * ATTN_SPLIT) == 0 else 1),
        out_shape=jax.ShapeDtypeStruct((B, T, MLA_HEADS * V_HEAD), BF16),
        grid=(B, MLA_HEADS, T // tq),
        in_specs=[pl.BlockSpec((1, tq, 256), lambda b, h, i: (b, i, h)),
                  pl.BlockSpec((1, S, 256), lambda b, h, i: (b, 0, h)),
                  pl.BlockSpec((1, S, V_HEAD), lambda b, h, i: (b, 0, h))],
        out_specs=pl.BlockSpec((1, tq, V_HEAD), lambda b, h, i: (b, i, h)),
        compiler_params=_cparams(3), name="mla_attention",
    )(q, k, v)


def _conv_body(prev_ref, cur_ref, next_ref, w_ref, o_ref, buf_ref, *, n_lat, n_tiles, l2, scale, tr):
    j = pl.program_id(1)
    tc = cur_ref.shape[2]
    prev_ok = jnp.logical_and(j != 0, j != n_lat)
    next_ok = jnp.logical_and(j != n_lat - 1, j != n_tiles - 1)
    zeros8 = jnp.zeros((8, tc), F32)
    buf_ref[0:8, :] = jnp.where(prev_ok, prev_ref[0].astype(F32)[CONV_HALO - 8:], zeros8)
    buf_ref[8:8 + tr, :] = cur_ref[0].astype(F32)
    buf_ref[8 + tr:16 + tr, :] = jnp.where(next_ok, next_ref[0].astype(F32)[:8], zeros8)
    w = w_ref[...]
    y = jnp.zeros((tr, tc), F32)
    for t in range(CONV_W):
        r0 = 8 + t - CONV_W // 2
        y = y + buf_ref[r0:r0 + tr, :] * w[t:t + 1, :]
    y = _silu(y)
    if l2:
        for h in range(tc // 128):
            yh = y[:, h * 128:(h + 1) * 128]
            yh = yh * lax.rsqrt(jnp.sum(yh * yh, axis=-1, keepdims=True) + EPS) * scale
            o_ref[0, :, h * 128:(h + 1) * 128] = yh.astype(o_ref.dtype)
    else:
        o_ref[0] = y.astype(o_ref.dtype)


def _short_conv(x, col0, width, w8, *, n_lat_rows, l2, scale=1.0):
    B, R, _ = x.shape
    tr = _pick_tile(int(np.gcd(n_lat_rows, R - n_lat_rows)) if R > n_lat_rows else n_lat_rows, 256, CONV_HALO)
    tc = _pick_tile(width, 1024)
    n_tiles, n_lat = R // tr, n_lat_rows // tr
    cb = col0 // tc
    hpt = tr // CONV_HALO
    n_halo = R // CONV_HALO
    assert col0 % tc == 0
    return pl.pallas_call(
        functools.partial(_conv_body, n_lat=n_lat, n_tiles=n_tiles, l2=l2, scale=scale, tr=tr),
        out_shape=jax.ShapeDtypeStruct((B, R, width), BF16),
        grid=(B, n_tiles, width // tc),
        in_specs=[pl.BlockSpec((1, CONV_HALO, tc), lambda b, j, c: (b, jnp.maximum(j * hpt - 1, 0), cb + c)),
                  pl.BlockSpec((1, tr, tc), lambda b, j, c: (b, j, cb + c)),
                  pl.BlockSpec((1, CONV_HALO, tc),
                               lambda b, j, c: (b, jnp.minimum((j + 1) * hpt, n_halo - 1), cb + c)),
                  pl.BlockSpec((8, tc), lambda b, j, c: (0, c))],
        out_specs=pl.BlockSpec((1, tr, tc), lambda b, j, c: (b, j, c)),
        scratch_shapes=[pltpu.VMEM((tr + 16, tc), F32)],
        compiler_params=_cparams(3), name="short_conv",
    )(x, x, x, w8)


def _dn_consts():
    C = DN_CHUNK
    r = np.arange(C)[:, None]
    c = np.arange(C)[None, :]
    tri = np.stack([(c <= r), (c >= r)]).astype(np.float32)
    ones = np.ones((C, C), np.float32)
    rhs = np.stack([np.concatenate([(r > c).astype(np.float32), ones], 1),
                    np.concatenate([(r < c).astype(np.float32), ones], 1)])
    rhs3 = np.concatenate([rhs, rhs, rhs], axis=1)
    incl = np.stack([(r >= c), (r <= c)]).astype(np.float32)
    strict = np.stack([(r > c), (r < c)]).astype(np.float32)
    levels = []
    s = 1
    while s < C:
        levels.append(((r // (2 * s) == c // (2 * s)) & (r // s != c // s)).astype(np.float32))
        s *= 2
    return (jnp.asarray(tri, BF16), jnp.asarray(rhs3, BF16), jnp.asarray(incl), jnp.asarray(strict),
            jnp.asarray(np.stack(levels)))


def _dn_chunks(ch, consts, with_q):
    C = DN_CHUNK
    tri_ref, rhs3_ref, incl_ref, strict_ref, lev_ref = consts
    n = range(len(ch))
    d = [c["d"] for c in ch]
    last = [C - 1 if di == 0 else 0 for di in d]
    la = [-c["a_scale"] * jax.nn.softplus(c["a_row"] + c["dt_b"]) for c in ch]
    beta = [jax.nn.sigmoid(c["b_row"]) for c in ch]
    dg = []
    for i in n:
        hi = la[i].astype(BF16)
        r1 = la[i] - hi.astype(F32)
        mid = r1.astype(BF16)
        lo = (r1 - mid.astype(F32)).astype(BF16)
        t = tri_ref[d[i]]
        dg.append(_bdot(jnp.concatenate([t * hi, t * mid, t * lo], axis=1), rhs3_ref[d[i]]))
    diff = [g[:, :C] for g in dg]
    gcol = [g[:, C:] for g in dg]
    tail_row = [jnp.exp(diff[i][last[i]:last[i] + 1, :]) for i in n]
    g_last = [jnp.exp(gcol[i][last[i]:last[i] + 1, :]) for i in n]
    dec = [incl_ref[d[i]] * jnp.exp(diff[i] * incl_ref[d[i]]) for i in n]
    egc = [jnp.exp(g) for g in gcol]
    gram = [_bdot(c["k"], c["kT"]) for c in ch]
    nd = [strict_ref[d[i]] * gram[i] * dec[i] * beta[i] for i in n]
    eye = incl_ref[0] * incl_ref[1]
    x = [eye - nd[i] * lev_ref[0] for i in n]
    for lv in range(1, lev_ref.shape[0]):
        xb = [xi.astype(BF16) for xi in x]
        m = [(nd[i] * lev_ref[lv]).astype(BF16) for i in n]
        xm = [_bdot(xb[i], m[i]).astype(BF16) for i in n]
        x = [x[i] - _bdot(xm[i], xb[i]) for i in n]
    ke = [(ch[i]["k"].astype(F32) * egc[i]).astype(BF16) for i in n]
    y = [_bdot(x[i].astype(BF16), jnp.concatenate([ch[i]["v"], ke[i]], axis=1)) for i in n]
    s_b = [c["S"].astype(BF16) for c in ch]
    if with_q:
        qg = [(ch[i]["q"].astype(F32) * egc[i]).astype(BF16) for i in n]
        r2 = [_bdot(jnp.concatenate([y[i][:, DN_DV:].astype(BF16), qg[i]], axis=0), s_b[i]) for i in n]
        yks = [r[:C] for r in r2]
        qs = [r[C:] for r in r2]
    else:
        yks = [_bdot(y[i][:, DN_DV:].astype(BF16), s_b[i]) for i in n]
    vt = [(y[i][:, :DN_DV] - yks[i]).astype(BF16) for i in n]
    kt_tail = [(ch[i]["kT"].astype(F32) * (beta[i] * tail_row[i])).astype(BF16) for i in n]
    s_new = [ch[i]["S"] * g_last[i] + _bdot(kt_tail[i], vt[i]) for i in n]
    if not with_q:
        return s_new, [None for _ in n]
    qk = [_bdot(c["q"], c["kT"]) for c in ch]
    p = [(qk[i] * dec[i] * beta[i]).astype(BF16) for i in n]
    o = [qs[i] + _bdot(p[i], vt[i]) for i in n]
    return s_new, o


def _dn_body(alog_ref, dtb_ref, k_ref, kT_ref, v_ref, q_ref, ab_ref, z_ref, g_ref,
             tri_ref, rhs3_ref, incl_ref, strict_ref, lev_ref, o_ref, acc_ref, s_ref, *, hb, n_lat, n_ctx):
    C = DN_CHUNK
    hg = pl.program_id(1)
    consts = (tri_ref, rhs3_ref, incl_ref, strict_ref, lev_ref)
    acc_ref[...] = jnp.zeros(acc_ref.shape, F32)
    s_ref[...] = jnp.zeros(s_ref.shape, F32)

    def run_chunk(cf, cb, with_q):
        ch, where = [], []
        for h in range(hb):
            head = hg * hb + h
            for d, c in ((0, cf), (1, cb)):
                row0 = pl.multiple_of(c * C, C)
                cols = slice(h * 128, (h + 1) * 128)
                ch.append(dict(
                    k=k_ref[0, pl.ds(row0, C), cols], v=v_ref[0, pl.ds(row0, C), cols], kT=kT_ref[0, h, c],
                    q=q_ref[0, pl.ds(row0, C), cols] if with_q else None,
                    a_row=ab_ref[0, h, d, pl.ds(c, 1), :], b_row=ab_ref[0, h, 2 + d, pl.ds(c, 1), :],
                    a_scale=jnp.exp(alog_ref[d, head]), dt_b=dtb_ref[d, head], S=s_ref[2 * h + d], d=d))
                where.append((row0, cols))
        s_new, o = _dn_chunks(ch, consts, with_q)
        for i, (row0, cols) in enumerate(where):
            s_ref[i] = s_new[i]
            if with_q:
                acc_ref[pl.ds(row0, C), cols] += o[i]

    @pl.loop(0, n_ctx)
    def _(n):
        run_chunk(n_lat + n, n_lat + n_ctx - 1 - n, False)

    @pl.loop(0, n_lat)
    def _(n):
        run_chunk(n, n_lat - 1 - n, True)

    rows = 256 if (n_lat * C) % 256 == 0 else C

    @pl.loop(0, (n_lat * C) // rows)
    def _(i):
        r0 = pl.multiple_of(i * rows, rows)
        for h in range(hb):
            cols = slice(h * 128, (h + 1) * 128)
            od = acc_ref[pl.ds(r0, rows), cols]
            zz = z_ref[0, pl.ds(r0, rows), cols].astype(F32)
            o_ref[0, pl.ds(r0, rows), cols] = (_rms(od, g_ref[...]) * zz).astype(o_ref.dtype)


def _deltanet(k, kT, v, q, ab, z, a_log, dt_bias, g_dn):
    B, S, _ = k.shape
    T = q.shape[1]
    C = DN_CHUNK
    assert C == DN_DK == DN_DV == V7X_LANES
    hb = DN_HEADS_PER_STEP
    n_lat, n_ctx = T // C, (S - T) // C
    consts = list(_dn_consts())
    smem = pl.BlockSpec(memory_space=pltpu.SMEM)
    one = pl.Buffered(1)

    def full(a):
        nd = a.ndim
        return pl.BlockSpec(a.shape, lambda b, g, _n=nd: (0,) * _n)

    return pl.pallas_call(
        functools.partial(_dn_body, hb=hb, n_lat=n_lat, n_ctx=n_ctx),
        out_shape=jax.ShapeDtypeStruct((B, T, DN_V), BF16),
        grid=(B, DN_HEADS // hb),
        in_specs=[smem, smem,
                  pl.BlockSpec((1, S, hb * 128), lambda b, g: (b, 0, g), pipeline_mode=one),
                  pl.BlockSpec((1, hb, S // C, 128, C), lambda b, g: (b, g, 0, 0, 0), pipeline_mode=one),
                  pl.BlockSpec((1, S, hb * 128), lambda b, g: (b, 0, g), pipeline_mode=one),
                  pl.BlockSpec((1, T, hb * 128), lambda b, g: (b, 0, g), pipeline_mode=one),
                  pl.BlockSpec((1, hb, 4, S // C, C), lambda b, g: (b, g, 0, 0, 0)),
                  pl.BlockSpec((1, T, hb * 128), lambda b, g: (b, 0, g), pipeline_mode=one),
                  pl.BlockSpec((1, 128), lambda b, g: (0, 0))] + [full(a) for a in consts],
        out_specs=pl.BlockSpec((1, T, hb * 128), lambda b, g: (b, 0, g)),
        scratch_shapes=[pltpu.VMEM((T, hb * 128), F32), pltpu.VMEM((2 * hb, DN_DK, DN_DV), F32)],
        compiler_params=_cparams(2), name="deltanet",
    )(a_log, dt_bias, k, kT, v, q, ab, z, g_dn, *consts)


def _merge_body(a_ref, b_ref, wa_ref, wb_ref, ga_ref, gb_ref, o_ref):
    oa = _bdot(a_ref[0], wa_ref[...])
    ob = _bdot(b_ref[0], wb_ref[...])
    o_ref[0] = (ga_ref[0].astype(F32) * oa + gb_ref[0].astype(F32) * ob).astype(o_ref.dtype)


def _merge(att, ob, w_oa, w_ob, gates):
    B, T, Ka = att.shape
    D = w_oa.shape[1]
    tm = _pick_tile(T, 1024, 8)
    tn = _pick_tile(D, 512)
    nj = D // tn
    return pl.pallas_call(
        _merge_body,
        out_shape=jax.ShapeDtypeStruct((B, T, D), BF16),
        grid=(B, T // tm, nj),
        in_specs=[pl.BlockSpec((1, tm, Ka), lambda b, i, j: (b, i, 0)),
                  pl.BlockSpec((1, tm, Ka), lambda b, i, j: (b, i, 0)),
                  pl.BlockSpec((Ka, tn), lambda b, i, j: (0, j)),
                  pl.BlockSpec((Ka, tn), lambda b, i, j: (0, j)),
                  pl.BlockSpec((1, tm, tn), lambda b, i, j: (b, i, j)),
                  pl.BlockSpec((1, tm, tn), lambda b, i, j: (b, i, j + nj))],
        out_specs=pl.BlockSpec((1, tm, tn), lambda b, i, j: (b, i, j)),
        compiler_params=_cparams(3), name="merge",
    )(att, ob, w_oa, w_ob, gates, gates)


def _norm_router_body(x_ref, g_ref, sh_ref, sc_ref, whl_ref, br_ref, h_ref, id_ref, gate_ref):
    h = _rms(x_ref[0], g_ref[...]) * (1.0 + sc_ref[0]) + sh_ref[0]
    half = h.shape[1] // 2
    h_ref[0] = _pack_bf16_pair(h[:, :half], h[:, half:])
    hh = h.astype(BF16)
    hl = (h - hh.astype(F32)).astype(BF16)
    r = _bdot(hh, whl_ref[...])
    logits = r[:, :128] + r[:, 128:] + _bdot(hl, whl_ref[:, :128]) + br_ref[...]
    lane = lax.broadcasted_iota(jnp.int32, logits.shape, 1)
    neg = jnp.float32(-jnp.inf)
    big = jnp.int32(1 << 20)
    is_g = lane < N_GROUPS
    lg = jnp.where(is_g, logits, neg)
    mg = jnp.max(lg, axis=-1, keepdims=True)
    eg = jnp.where(is_g, jnp.exp(lg - mg), 0.0)
    pg = eg / jnp.sum(eg, axis=-1, keepdims=True)
    pg_top = jnp.max(pg, axis=-1, keepdims=True)
    g_top = jnp.min(jnp.where(jnp.logical_and(is_g, pg == pg_top), lane, big), axis=-1, keepdims=True)
    lo = N_GROUPS + EXPERTS_PER_GROUP * g_top
    is_e = jnp.logical_and(lane >= lo, lane < lo + EXPERTS_PER_GROUP)
    le = jnp.where(is_e, logits, neg)
    me = jnp.max(le, axis=-1, keepdims=True)
    ee = jnp.where(is_e, jnp.exp(le - me), 0.0)
    pe = ee / jnp.sum(ee, axis=-1, keepdims=True)
    p1 = jnp.max(pe, axis=-1, keepdims=True)
    i1 = jnp.min(jnp.where(jnp.logical_and(is_e, pe == p1), lane, big), axis=-1, keepdims=True)
    rest = jnp.logical_and(is_e, lane != i1)
    pe2 = jnp.where(rest, pe, -1.0)
    p2 = jnp.max(pe2, axis=-1, keepdims=True)
    i2 = jnp.min(jnp.where(jnp.logical_and(rest, pe2 == p2), lane, big), axis=-1, keepdims=True)
    den = p1 + p2
    id_ref[0] = jnp.where(lane == 0, i1 - N_GROUPS, jnp.where(lane == 1, i2 - N_GROUPS, 0))
    gate_ref[0] = jnp.where(lane == 0, p1 / den * pg_top, jnp.where(lane == 1, p2 / den * pg_top, 0.0))


def _norm_router(x, g, shift, scale, w_hl, b_r):
    B, T, D = x.shape
    tr = _pick_tile(T, 256, 8)
    return pl.pallas_call(
        _norm_router_body,
        out_shape=(jax.ShapeDtypeStruct((B, T, D // 2), jnp.uint32),
                   jax.ShapeDtypeStruct((B, T, 128), jnp.int32),
                   jax.ShapeDtypeStruct((B, T, 128), F32)),
        grid=(B, T // tr),
        in_specs=[pl.BlockSpec((1, tr, D), lambda b, i: (b, i, 0)),
                  pl.BlockSpec((1, D), lambda b, i: (0, 0)),
                  pl.BlockSpec((1, 1, D), lambda b, i: (b, 0, 0)),
                  pl.BlockSpec((1, 1, D), lambda b, i: (b, 0, 0)),
                  pl.BlockSpec((D, 256), lambda b, i: (0, 0)),
                  pl.BlockSpec((1, 128), lambda b, i: (0, 0))],
        out_specs=(pl.BlockSpec((1, tr, D // 2), lambda b, i: (b, i, 0)),
                   pl.BlockSpec((1, tr, 128), lambda b, i: (b, i, 0)),
                   pl.BlockSpec((1, tr, 128), lambda b, i: (b, i, 0))),
        compiler_params=_cparams(2), name="norm_router",
    )(x, g, shift, scale, w_hl, b_r)


def _row_gather(idx_ref, src_ref, buf_ref, sem, slot, n_rows):
    def row_copy(r, s):
        return pltpu.make_async_copy(src_ref.at[pl.ds(s, 1)], buf_ref.at[slot, pl.ds(r, 1)], sem.at[slot])

    def start():
        def body(r, c):
            row_copy(r, idx_ref[0, 0, r]).start()
            return c
        lax.fori_loop(0, n_rows, body, 0, unroll=8)

    def wait():
        def body(r, c):
            row_copy(r, 0).wait()
            return c
        lax.fori_loop(0, n_rows, body, 0, unroll=8)

    return start, wait


def _moe_gather_body(nu_ref, src_ref, nxt_ref, h_ref, o_ref, buf_ref, sem):
    i = pl.program_id(0)
    slot = lax.rem(i, 2)
    start_cur, wait_cur = _row_gather(src_ref, h_ref, buf_ref, sem, slot, MOE_ROWS)
    start_nxt, _ = _row_gather(nxt_ref, h_ref, buf_ref, sem, 1 - slot, MOE_ROWS)

    @pl.when(jnp.logical_and(i == 0, nu_ref[0] > 0))
    def _():
        start_cur()

    @pl.when(i + 1 < nu_ref[0])
    def _():
        start_nxt()

    @pl.when(i < nu_ref[0])
    def _():
        wait_cur()
        lo, hi = _unpack_bf16_pair(buf_ref[slot])
        half = lo.shape[1]
        o_ref[:, :half] = lo.astype(o_ref.dtype)
        o_ref[:, half:] = hi.astype(o_ref.dtype)

    @pl.when(i >= nu_ref[0])
    def _():
        o_ref[...] = jnp.zeros(o_ref.shape, o_ref.dtype)


def _moe_gather(h, src, n_used):
    n, D = h.shape[0], 2 * h.shape[1]
    R = src.shape[0]
    nb = R // MOE_ROWS
    src3 = src.reshape(nb, 1, MOE_ROWS)
    return pl.pallas_call(
        _moe_gather_body,
        out_shape=jax.ShapeDtypeStruct((R, D), BF16),
        grid_spec=pltpu.PrefetchScalarGridSpec(
            num_scalar_prefetch=1, grid=(nb,),
            in_specs=[pl.BlockSpec((1, 1, MOE_ROWS), lambda i, nu: (i, 0, 0), memory_space=pltpu.SMEM),
                      pl.BlockSpec((1, 1, MOE_ROWS), lambda i, nu: (jnp.minimum(i + 1, nb - 1), 0, 0),
                                   memory_space=pltpu.SMEM),
                      pl.BlockSpec(memory_space=pl.ANY)],
            out_specs=pl.BlockSpec((MOE_ROWS, D), lambda i, nu: (i, 0)),
            scratch_shapes=[pltpu.VMEM((2, MOE_ROWS, D // 2), jnp.uint32), pltpu.SemaphoreType.DMA((2,))]),
        compiler_params=_cparams(1), name="moe_gather",
    )(n_used, src3, src3, h)


def _moe_up_body(be_ref, nu_ref, x_ref, w1_ref, w3_ref, o_ref, w1b_ref, w3b_ref):
    i = pl.program_id(1)
    changed = jnp.logical_or(i == 0, be_ref[i] != be_ref[jnp.maximum(i - 1, 0)])

    @pl.when(changed)
    def _():
        w1b_ref[...] = w1_ref[0].astype(BF16)
        w3b_ref[...] = w3_ref[0].astype(BF16)

    @pl.when(i < nu_ref[0])
    def _():
        x = x_ref[...]
        a = _bdot(x, w1b_ref[...])
        b = _bdot(x, w3b_ref[...])
        o_ref[...] = (_silu(a) * b).astype(o_ref.dtype)

    @pl.when(i >= nu_ref[0])
    def _():
        o_ref[...] = jnp.zeros(o_ref.shape, o_ref.dtype)


def _moe_down_body(be_ref, nu_ref, h_ref, w2_ref, o_ref, w2b_ref):
    i = pl.program_id(1)
    changed = jnp.logical_or(i == 0, be_ref[i] != be_ref[jnp.maximum(i - 1, 0)])

    @pl.when(changed)
    def _():
        w2b_ref[...] = w2_ref[0].astype(BF16)

    @pl.when(i < nu_ref[0])
    def _():
        y = _bdot(h_ref[...], w2b_ref[...])
        half = y.shape[1] // 2
        o_ref[...] = _pack_bf16_pair(y[:, :half], y[:, half:])

    @pl.when(i >= nu_ref[0])
    def _():
        o_ref[...] = jnp.zeros(o_ref.shape, o_ref.dtype)


def _moe_experts(x_pad, blk_expert, n_used, w1, w3, w2):
    R, D = x_pad.shape
    nb = R // MOE_ROWS
    Hd = w1.shape[2]
    th = _pick_tile(Hd, 512)
    hbuf = pl.pallas_call(
        _moe_up_body,
        out_shape=jax.ShapeDtypeStruct((R, Hd), BF16),
        grid_spec=pltpu.PrefetchScalarGridSpec(
            num_scalar_prefetch=2, grid=(Hd // th, nb),
            in_specs=[pl.BlockSpec((MOE_ROWS, D), lambda j, i, be, nu: (i, 0)),
                      pl.BlockSpec((1, D, th), lambda j, i, be, nu: (be[i], 0, j)),
                      pl.BlockSpec((1, D, th), lambda j, i, be, nu: (be[i], 0, j))],
            out_specs=pl.BlockSpec((MOE_ROWS, th), lambda j, i, be, nu: (i, j)),
            scratch_shapes=[pltpu.VMEM((D, th), BF16), pltpu.VMEM((D, th), BF16)]),
        compiler_params=_cparams(2), name="moe_up",
    )(blk_expert, n_used, x_pad, w1, w3)
    tn = _pick_tile(D, MOE_DOWN_COLS)
    return pl.pallas_call(
        _moe_down_body,
        out_shape=jax.ShapeDtypeStruct((R, D // 2), jnp.uint32),
        grid_spec=pltpu.PrefetchScalarGridSpec(
            num_scalar_prefetch=2, grid=(D // tn, nb),
            in_specs=[pl.BlockSpec((MOE_ROWS, Hd), lambda j, i, be, nu: (i, 0)),
                      pl.BlockSpec((1, Hd, tn), lambda j, i, be, nu: (be[i], 0, j))],
            out_specs=pl.BlockSpec((MOE_ROWS, tn // 2), lambda j, i, be, nu: (i, j)),
            scratch_shapes=[pltpu.VMEM((Hd, tn), BF16)]),
        compiler_params=_cparams(2), name="moe_down",
    )(blk_expert, n_used, hbuf, w2)


def _combine_body(dest_ref, nxt_ref, x_ref, p_ref, g_ref, y_ref, o_ref, buf_ref, sem, *, tr, n_steps, tn):
    t = pl.program_id(0)
    slot = lax.rem(t, 2)
    start_cur, wait_cur = _row_gather(dest_ref, y_ref, buf_ref, sem, slot, TOP_K * tr)
    start_nxt, _ = _row_gather(nxt_ref, y_ref, buf_ref, sem, 1 - slot, TOP_K * tr)

    @pl.when(t == 0)
    def _():
        start_cur()

    @pl.when(t + 1 < n_steps)
    def _():
        start_nxt()

    wait_cur()
    p = p_ref[0]
    half = tn // 2
    for j in range(x_ref.shape[2] // tn):
        words = slice(j * half, (j + 1) * half)
        lo0, hi0 = _unpack_bf16_pair(buf_ref[slot, 0:tr, words])
        lo1, hi1 = _unpack_bf16_pair(buf_ref[slot, tr:2 * tr, words])
        for y0, y1, c0 in ((lo0, lo1, j * tn), (hi0, hi1, j * tn + half)):
            cols = slice(c0, c0 + half)
            ffn = p[:, 0:1] * y0 + p[:, 1:2] * y1
            o_ref[0, :, cols] = x_ref[0, :, cols] + g_ref[0, :, cols] * ffn


def _combine(x, y_pad, dest, gate, g2):
    B, T, D = x.shape
    tr = _pick_tile(T, 256, 8)
    nt = T // tr
    n_steps = B * nt
    dest_t = dest.reshape(n_steps, tr, TOP_K).transpose(0, 2, 1).reshape(n_steps, 1, TOP_K * tr)
    row = pl.BlockSpec((1, tr, D), lambda t: (t // nt, t % nt, 0))
    return pl.pallas_call(
        functools.partial(_combine_body, tr=tr, n_steps=n_steps, tn=_pick_tile(D, MOE_DOWN_COLS)),
        out_shape=jax.ShapeDtypeStruct((B, T, D), F32),
        grid=(n_steps,),
        in_specs=[pl.BlockSpec((1, 1, TOP_K * tr), lambda t: (t, 0, 0), memory_space=pltpu.SMEM),
                  pl.BlockSpec((1, 1, TOP_K * tr), lambda t: (jnp.minimum(t + 1, n_steps - 1), 0, 0),
                               memory_space=pltpu.SMEM),
                  row,
                  pl.BlockSpec((1, tr, 128), lambda t: (t // nt, t % nt, 0)),
                  pl.BlockSpec((1, 1, D), lambda t: (t // nt, 0, 0)),
                  pl.BlockSpec(memory_space=pl.ANY)],
        out_specs=row,
        scratch_shapes=[pltpu.VMEM((2, TOP_K * tr, D // 2), jnp.uint32), pltpu.SemaphoreType.DMA((2,))],
        compiler_params=_cparams(1), name="moe_combine",
    )(dest_t, dest_t, x, gate, g2, y_pad)


def _rope_tables(T, Tc):
    rows = T // GRID_W
    row = jnp.repeat(jnp.arange(rows), GRID_W).astype(F32)
    col = jnp.tile(jnp.arange(GRID_W), rows).astype(F32)
    half = QK_ROPE // 2
    inv_freq = ROPE_THETA ** (-jnp.arange(0, half, 2, dtype=F32) / half)
    ang = jnp.concatenate([row[:, None] * inv_freq, col[:, None] * inv_freq], axis=-1)
    cos, sin = jnp.cos(ang), jnp.sin(ang)
    z64 = jnp.zeros((T, 64), F32)
    lat = jnp.concatenate([cos, cos, z64, -sin, sin, z64], axis=-1)
    ctx = jnp.concatenate([jnp.ones((Tc, 64), F32), jnp.zeros((Tc, 192), F32)], axis=-1)
    return jnp.concatenate([lat, ctx], axis=0)


_PERM = np.concatenate([np.arange(0, QK_ROPE, 2), np.arange(1, QK_ROPE, 2)])
_PERM_SW = np.concatenate([np.arange(1, QK_ROPE, 2), np.arange(0, QK_ROPE, 2)])


def _pad_cols(a, n):
    return jnp.pad(a, ((0, 0), (0, n - a.shape[1])))


def _moe_dispatch(expert_ids, n_tokens):
    a = n_tokens * TOP_K
    e_flat = expert_ids.reshape(a)
    onehot = (e_flat[:, None] == jnp.arange(N_EXPERTS)[None, :]).astype(F32)
    ch = 128 if a % 128 == 0 else a
    nc = a // ch
    hp = lax.Precision.HIGHEST
    within = jnp.einsum("ij,cje->cie", jnp.tril(jnp.ones((ch, ch), F32)), onehot.reshape(nc, ch, N_EXPERTS),
                        precision=hp)
    before = jnp.einsum("ij,je->ie", jnp.tril(jnp.ones((nc, nc), F32), -1), within[:, -1, :], precision=hp)
    csum = (within + before[:, None, :]).reshape(a, N_EXPERTS)
    rank = jnp.sum(csum * onehot, axis=1).astype(jnp.int32) - 1
    counts = csum[-1].astype(jnp.int32)
    padded = (counts + MOE_ROWS - 1) // MOE_ROWS * MOE_ROWS
    pad_end = jnp.cumsum(padded)
    pad_start = pad_end - padded
    dest = pad_start[e_flat] + rank
    n_blocks = -(-a // MOE_ROWS) + N_EXPERTS
    src = jnp.zeros((n_blocks * MOE_ROWS,), jnp.int32).at[dest].set(jnp.arange(a, dtype=jnp.int32) // TOP_K)
    blk_expert = jnp.minimum(
        jnp.searchsorted(pad_end, jnp.arange(n_blocks) * MOE_ROWS, side="right"), N_EXPERTS - 1).astype(jnp.int32)
    n_used = (pad_end[-1] // MOE_ROWS).astype(jnp.int32).reshape(1)
    return dest, src, blk_expert, n_used


def _layer(x, c, ctx, c_ctx, p):
    B, T, D = x.shape
    Tc = ctx.shape[1]
    S = T + Tc
    H = MLA_HEADS

    cin = jnp.zeros((1, 16, D), F32).at[0, :B].set(c).at[0, B].set(c_ctx)
    mod = _matmul(cin, p["w_mod"], tm=16, tn=1024, out_dtype=F32, pre=_silu,
                  epi=lambda acc, b: acc + b, extras=[(p["b_mod"].reshape(1, 1, -1), "bcol")], name="mod")[0]
    mod_l = mod[:B].reshape(B, N_MOD, 1, D)
    mod_c = mod[B].reshape(N_MOD, 1, 1, D)

    h = _norm_mod(x, ctx, p["norm1_g"].reshape(1, D), mod_l[:, 0], mod_l[:, 1], mod_c[0], mod_c[1])

    w_in = p["w_in"]
    wkr = w_in[:, OFF_KR:OFF_AB]
    w_small = jnp.concatenate([w_in[:, OFF_CKV:OFF_KR], _pad_cols(wkr[:, _PERM], 128),
                               _pad_cols(wkr[:, _PERM_SW], 128), _pad_cols(w_in[:, OFF_AB:OFF_DK], 128)],
                              axis=1).astype(BF16)
    p_small = _matmul(h, w_small, tm=1152, tn=896, out_dtype=F32, name="in_small")
    p_dkv = _matmul(h, w_in[:, OFF_DK:OFF_DQ].astype(BF16), tm=1152, name="in_dkv")
    p_dq = _matmul(h, w_in[:, OFF_DQ:OFF_CQ].astype(BF16), rows=T, name="in_dq")
    p_cq = _matmul(h, w_in[:, OFF_CQ:OFF_Z].astype(BF16), rows=T, out_dtype=F32, name="in_cq")
    z_act = _matmul(h, w_in[:, OFF_Z:OFF_GATE].astype(BF16), rows=T, epi=_silu, name="in_z")
    gates = _matmul(h, w_in[:, OFF_GATE:].astype(BF16), rows=T, epi=jax.nn.sigmoid, name="in_gate")

    tab = _rope_tables(T, Tc)
    gkr = p["k_rope_norm_g"]
    ckv_n, k_rope = _mla_small(p_small, tab, p["kv_norm_g"].reshape(1, -1),
                               _pad_cols(gkr[_PERM][None], 128), _pad_cols(gkr[_PERM_SW][None], 128))
    w_ukv = p["w_ukv"].reshape(KV_LORA, H, QK_NOPE + V_HEAD)
    hg = HEAD_GROUP
    w_kv_perm = jnp.concatenate(
        [w_ukv[:, :, :QK_NOPE].reshape(KV_LORA, H // hg, hg * QK_NOPE),
         w_ukv[:, :, QK_NOPE:].reshape(KV_LORA, H // hg, hg * V_HEAD)], axis=2).reshape(KV_LORA, H * 256).astype(BF16)
    k_all, v_all = _kv_up(ckv_n, w_kv_perm, k_rope, p["k_norm_g"].reshape(1, -1))

    w_uq = p["w_uq"].reshape(Q_LORA, H, QK_NOPE + QK_ROPE)
    wq_r = w_uq[:, :, QK_NOPE:]
    zpad = jnp.zeros((Q_LORA, H, 64), F32)
    w_q_perm = jnp.concatenate(
        [w_uq[:, :, :QK_NOPE].reshape(Q_LORA, H // hg, hg * 128),
         jnp.concatenate([wq_r[:, :, _PERM], zpad], axis=2).reshape(Q_LORA, H // hg, hg * 128),
         jnp.concatenate([wq_r[:, :, _PERM_SW], zpad], axis=2).reshape(Q_LORA, H // hg, hg * 128)],
        axis=2).reshape(Q_LORA, H * 384).astype(BF16)
    gqr = p["q_rope_norm_g"]
    q_all = _q_up(p_cq, w_q_perm, tab[:T], p["q_norm_g"].reshape(1, -1), p["q_a_norm_g"].reshape(1, -1),
                  _pad_cols(gqr[_PERM][None], 128), _pad_cols(gqr[_PERM_SW][None], 128))
    att = _attention(q_all, k_all, v_all)

    cw = jnp.pad(p["conv_w"], ((0, 8 - CONV_W), (0, 0)))
    dn_k = _short_conv(p_dkv, 0, DN_QK, cw[:, :DN_QK], n_lat_rows=T, l2=True)
    dn_v = _short_conv(p_dkv, DN_QK, DN_V, cw[:, DN_QK:DN_QK + DN_V], n_lat_rows=T, l2=False)
    dn_q = _short_conv(p_dq, 0, DN_QK, cw[:, DN_QK + DN_V:], n_lat_rows=T, l2=True, scale=float(DN_DK ** -0.5))
    C = DN_CHUNK
    dn_kT = dn_k.reshape(B, S // C, C, DN_HEADS, DN_DK).transpose(0, 3, 1, 4, 2)
    ab = p_small[:, :, 768:768 + 4 * DN_HEADS].reshape(B, S // C, C, 4, DN_HEADS).transpose(0, 4, 3, 1, 2)
    ob_in = _deltanet(dn_k, dn_kT, dn_v, dn_q, ab, z_act, p["a_log"], p["dt_bias"], p["dn_norm_g"].reshape(1, -1))

    mix = _merge(att, ob_in, p["w_oa"].astype(BF16), p["w_ob"].astype(BF16), gates)
    xl = _matmul(mix, p["w_out"].astype(BF16), out_dtype=F32,
                 epi=lambda acc, xr, g: xr + g * acc, extras=[(x, "tile"), (mod_l[:, 2], "bcol")], name="out_proj")

    w_r = _pad_cols(jnp.concatenate([p["w_rg"], p["w_re"]], axis=1), 128)
    w_r_hi = w_r.astype(BF16)
    w_r_lo = (w_r - w_r_hi.astype(F32)).astype(BF16)
    b_r = _pad_cols(jnp.concatenate([p["b_rg"], p["b_re"]])[None], 128)
    h2, ids, gate = _norm_router(xl, p["norm2_g"].reshape(1, D), mod_l[:, 3], mod_l[:, 4],
                                 jnp.concatenate([w_r_hi, w_r_lo], axis=1), b_r)
    n = B * T
    dest, src, blk_expert, n_used = _moe_dispatch(ids[:, :, :TOP_K].reshape(n, TOP_K), n)
    x_pad = _moe_gather(h2.reshape(n, D // 2), src, n_used)
    y_pad = _moe_experts(x_pad, blk_expert, n_used, p["w1"], p["w3"], p["w2"])
    return _combine(xl, y_pad, dest.reshape(n, TOP_K), gate, mod_l[:, 5])


def kernel(x, c, ctx, c_ctx, norm1_g, norm2_g, w_mod, b_mod, w_in, q_a_norm_g, w_uq, kv_norm_g, w_ukv, q_norm_g,
           q_rope_norm_g, k_norm_g, k_rope_norm_g, conv_w, a_log, dt_bias, dn_norm_g, w_oa, w_ob, w_out, w_rg, b_rg,
           w_re, b_re, w1, w3, w2):
    depth = norm1_g.shape[0]
    assert depth == 1, "single-layer block: the context stream is read, never updated"
    layer = 0
    p = {
        "norm1_g": norm1_g[layer], "norm2_g": norm2_g[layer], "w_mod": w_mod[layer], "b_mod": b_mod[layer],
        "w_in": w_in[layer], "q_a_norm_g": q_a_norm_g[layer], "w_uq": w_uq[layer], "kv_norm_g": kv_norm_g[layer],
        "w_ukv": w_ukv[layer], "q_norm_g": q_norm_g[layer], "q_rope_norm_g": q_rope_norm_g[layer],
        "k_norm_g": k_norm_g[layer], "k_rope_norm_g": k_rope_norm_g[layer], "conv_w": conv_w[layer],
        "a_log": a_log[layer], "dt_bias": dt_bias[layer], "dn_norm_g": dn_norm_g[layer], "w_oa": w_oa[layer],
        "w_ob": w_ob[layer], "w_out": w_out[layer], "w_rg": w_rg[layer], "b_rg": b_rg[layer], "w_re": w_re[layer],
        "b_re": b_re[layer], "w1": w1[layer], "w3": w3[layer], "w2": w2[layer],
    }
    return _layer(x, c, ctx, c_ctx, p)
```

```python
import functools

import numpy as np
import jax
import jax.numpy as jnp
from jax import lax
from jax.experimental import pallas as pl
from jax.experimental.pallas import tpu as pltpu

F32 = jnp.float32
BF16 = jnp.bfloat16

EPS = 1e-6
GRID_W = 64
N_MOD = 6
MLA_HEADS = 16
Q_LORA = 1024
KV_LORA = 512
QK_NOPE = 128
QK_ROPE = 64
V_HEAD = 128
ROPE_THETA = 10000.0
DN_HEADS = 16
DN_DK = 128
DN_DV = 128
DN_QK = DN_HEADS * DN_DK
DN_V = DN_HEADS * DN_DV
CONV_W = 5
N_GROUPS = 4
EXPERTS_PER_GROUP = 8
N_EXPERTS = N_GROUPS * EXPERTS_PER_GROUP
TOP_K = 2
EXPERT_HIDDEN = 1024

OFF_CKV = 0
OFF_KR = OFF_CKV + KV_LORA
OFF_AB = OFF_KR + QK_ROPE
OFF_DK = OFF_AB + 4 * DN_HEADS
OFF_DV = OFF_DK + DN_QK
OFF_DQ = OFF_DV + DN_V
OFF_CQ = OFF_DQ + DN_QK
OFF_Z = OFF_CQ + Q_LORA
OFF_GATE = OFF_Z + DN_V

V7X_LANES = 128
V7X_VMEM_BYTES = 64 * 1024 * 1024
VMEM_LIMIT = V7X_VMEM_BYTES - 8 * 1024 * 1024

DN_CHUNK = 128
MOE_ROWS = 512
MOE_DOWN_COLS = 2048
ROW_DMA_GROUP = 8
HEAD_GROUP = 4
DN_HEADS_PER_STEP = 8
ATTN_ROWS = 1024
ATTN_SPLIT = 4
CONV_HALO = 16


def _cparams(n_grid):
    return pltpu.CompilerParams(dimension_semantics=("arbitrary",) * n_grid, vmem_limit_bytes=VMEM_LIMIT)


def _pick_tile(n, target, mult=V7X_LANES):
    if n <= target:
        return n
    best = None
    t = mult
    while t <= target:
        if n % t == 0:
            best = t
        t += mult
    assert best is not None, (n, target)
    return best


def _rms(x, g):
    return x * lax.rsqrt(jnp.mean(x * x, axis=-1, keepdims=True) + EPS) * g


def _silu(x):
    return x * jax.nn.sigmoid(x)


def _bdot(a, b):
    return jnp.dot(a, b, preferred_element_type=F32)


def _pack_bf16_pair(lo, hi):
    lo_bits = lax.bitcast_convert_type(lo.astype(BF16).astype(F32), jnp.uint32)
    hi_bits = lax.bitcast_convert_type(hi.astype(BF16).astype(F32), jnp.uint32)
    return (lo_bits >> 16) | (hi_bits & jnp.uint32(0xFFFF0000))


def _unpack_bf16_pair(w):
    lo = lax.bitcast_convert_type(w << 16, F32)
    hi = lax.bitcast_convert_type(w & jnp.uint32(0xFFFF0000), F32)
    return lo, hi


def _mm_body(*refs, pre, epi, n_extra, cached):
    x_ref, w_ref = refs[0], refs[1]
    extra = refs[2:2 + n_extra]
    o_ref = refs[2 + n_extra]
    if cached:
        xs_ref = refs[3 + n_extra]

        @pl.when(pl.program_id(2) == 0)
        def _():
            xs_ref[...] = pre(x_ref[0]).astype(BF16)

        x = xs_ref[...]
    else:
        x = x_ref[0]
    w = w_ref[...]
    if w.dtype != BF16:
        w = w.astype(BF16)
    acc = _bdot(x, w)
    if epi is not None:
        acc = epi(acc, *[e[0] for e in extra])
    o_ref[0] = acc.astype(o_ref.dtype)


def _matmul(x, w, *, rows=None, tm=1024, tn=512, out_dtype=BF16, pre=None, epi=None, extras=(), name="mm"):
    B, S, K = x.shape
    N = w.shape[1]
    rows = S if rows is None else rows
    tm = _pick_tile(rows, tm)
    tn = _pick_tile(N, tn)
    grid = (B, rows // tm, N // tn)
    in_specs = [pl.BlockSpec((1, tm, K), lambda b, i, j: (b, i, 0)),
                pl.BlockSpec((K, tn), lambda b, i, j: (0, j))]
    args = [x, w]
    for arr, kind in extras:
        if kind == "tile":
            in_specs.append(pl.BlockSpec((1, tm, tn), lambda b, i, j: (b, i, j)))
        elif kind == "bcol":
            if arr.shape[0] == 1:
                in_specs.append(pl.BlockSpec((1, 1, tn), lambda b, i, j: (0, 0, j)))
            else:
                in_specs.append(pl.BlockSpec((1, 1, tn), lambda b, i, j: (b, 0, j)))
        else:
            raise ValueError(kind)
        args.append(arr)
    cached = pre is not None
    scratch = [pltpu.VMEM((tm, K), BF16)] if cached else []
    if not cached:
        assert x.dtype == BF16
    return pl.pallas_call(
        functools.partial(_mm_body, pre=pre, epi=epi, n_extra=len(extras), cached=cached),
        out_shape=jax.ShapeDtypeStruct((B, rows, N), out_dtype),
        grid=grid, in_specs=in_specs,
        out_specs=pl.BlockSpec((1, tm, tn), lambda b, i, j: (b, i, j)),
        scratch_shapes=scratch, compiler_params=_cparams(3), name=name,
    )(*args)


def _norm_mod_body(x_ref, c_ref, g_ref, sl_ref, scl_ref, sc_ref, scc_ref, o_ref, *, n_lat):
    j = pl.program_id(1)

    @pl.when(j < n_lat)
    def _():
        h = _rms(x_ref[0], g_ref[...])
        o_ref[0] = (h * (1.0 + scl_ref[0]) + sl_ref[0]).astype(o_ref.dtype)

    @pl.when(j >= n_lat)
    def _():
        h = _rms(c_ref[0], g_ref[...])
        o_ref[0] = (h * (1.0 + scc_ref[0]) + sc_ref[0]).astype(o_ref.dtype)


def _norm_mod(x, ctx, g, shift_l, scale_l, shift_c, scale_c):
    B, T, D = x.shape
    Tc = ctx.shape[1]
    tr = _pick_tile(int(np.gcd(T, Tc)), 256, 8)
    n_lat, n_ctx = T // tr, Tc // tr
    return pl.pallas_call(
        functools.partial(_norm_mod_body, n_lat=n_lat),
        out_shape=jax.ShapeDtypeStruct((B, T + Tc, D), BF16),
        grid=(B, n_lat + n_ctx),
        in_specs=[pl.BlockSpec((1, tr, D), lambda b, j: (b, jnp.minimum(j, n_lat - 1), 0)),
                  pl.BlockSpec((1, tr, D), lambda b, j: (b, jnp.maximum(j - n_lat, 0), 0)),
                  pl.BlockSpec((1, D), lambda b, j: (0, 0)),
                  pl.BlockSpec((1, 1, D), lambda b, j: (b, 0, 0)),
                  pl.BlockSpec((1, 1, D), lambda b, j: (b, 0, 0)),
                  pl.BlockSpec((1, 1, D), lambda b, j: (0, 0, 0)),
                  pl.BlockSpec((1, 1, D), lambda b, j: (0, 0, 0))],
        out_specs=pl.BlockSpec((1, tr, D), lambda b, j: (b, j, 0)),
        compiler_params=_cparams(2), name="norm_mod",
    )(x, ctx, g, shift_l, scale_l, shift_c, scale_c)


def _mla_small_body(p_ref, tab_ref, gkv_ref, gkr_ref, gks_ref, ckv_ref, kr_ref):
    p = p_ref[0]
    ckv_ref[0] = _rms(p[:, :KV_LORA], gkv_ref[...]).astype(ckv_ref.dtype)
    xr = p[:, KV_LORA:KV_LORA + 128]
    xs = p[:, KV_LORA + 128:KV_LORA + 256]
    inv = lax.rsqrt(jnp.sum(xr * xr, axis=-1, keepdims=True) * (1.0 / QK_ROPE) + EPS)
    tab = tab_ref[...]
    kr = inv * (xr * gkr_ref[...] * tab[:, :128] + xs * gks_ref[...] * tab[:, 128:])
    kr_ref[0] = kr.astype(kr_ref.dtype)


def _mla_small(p_small, tab, g_kv, g_kr, g_ks):
    B, S, W = p_small.shape
    tr = _pick_tile(S, 768, 8)
    return pl.pallas_call(
        _mla_small_body,
        out_shape=(jax.ShapeDtypeStruct((B, S, KV_LORA), BF16), jax.ShapeDtypeStruct((B, S, 128), BF16)),
        grid=(B, S // tr),
        in_specs=[pl.BlockSpec((1, tr, 768), lambda b, i: (b, i, 0)),
                  pl.BlockSpec((tr, 256), lambda b, i: (i, 0)),
                  pl.BlockSpec((1, KV_LORA), lambda b, i: (0, 0)),
                  pl.BlockSpec((1, 128), lambda b, i: (0, 0)),
                  pl.BlockSpec((1, 128), lambda b, i: (0, 0))],
        out_specs=(pl.BlockSpec((1, tr, KV_LORA), lambda b, i: (b, i, 0)),
                   pl.BlockSpec((1, tr, 128), lambda b, i: (b, i, 0))),
        compiler_params=_cparams(2), name="mla_small",
    )(p_small, tab, g_kv, g_kr, g_ks)


def _kv_up_body(x_ref, w_ref, kr_ref, gk_ref, k_ref, v_ref, *, hg):
    acc = _bdot(x_ref[0], w_ref[...])
    kr = kr_ref[0]
    for h in range(hg):
        kn = _rms(acc[:, h * 128:(h + 1) * 128], gk_ref[...])
        k_ref[0, :, h * 256:h * 256 + 128] = kn.astype(k_ref.dtype)
        k_ref[0, :, h * 256 + 128:(h + 1) * 256] = kr
    v_ref[0] = acc[:, hg * 128:].astype(v_ref.dtype)


def _kv_up(ckv, w_perm, kr, g_k):
    B, S, _ = ckv.shape
    hg = HEAD_GROUP
    tm = _pick_tile(S, 1152, 8)
    return pl.pallas_call(
        functools.partial(_kv_up_body, hg=hg),
        out_shape=(jax.ShapeDtypeStruct((B, S, MLA_HEADS * 256), BF16),
                   jax.ShapeDtypeStruct((B, S, MLA_HEADS * V_HEAD), BF16)),
        grid=(B, S // tm, MLA_HEADS // hg),
        in_specs=[pl.BlockSpec((1, tm, KV_LORA), lambda b, i, j: (b, i, 0)),
                  pl.BlockSpec((KV_LORA, hg * 256), lambda b, i, j: (0, j)),
                  pl.BlockSpec((1, tm, 128), lambda b, i, j: (b, i, 0)),
                  pl.BlockSpec((1, 128), lambda b, i, j: (0, 0))],
        out_specs=(pl.BlockSpec((1, tm, hg * 256), lambda b, i, j: (b, i, j)),
                   pl.BlockSpec((1, tm, hg * 128), lambda b, i, j: (b, i, j))),
        compiler_params=_cparams(3), name="kv_up",
    )(ckv, w_perm, kr, g_k)


def _q_up_body(x_ref, w_ref, tab_ref, gq_ref, gqa_ref, gr_ref, gs_ref, q_ref, xs_ref, *, hg, scale):
    @pl.when(pl.program_id(2) == 0)
    def _():
        xs_ref[...] = _rms(x_ref[0], gqa_ref[...]).astype(BF16)

    acc = _bdot(xs_ref[...], w_ref[...])
    tab = tab_ref[...]
    for h in range(hg):
        qn = _rms(acc[:, h * 128:(h + 1) * 128], gq_ref[...]) * scale
        xr = acc[:, (hg + h) * 128:(hg + h + 1) * 128]
        xw = acc[:, (2 * hg + h) * 128:(2 * hg + h + 1) * 128]
        inv = lax.rsqrt(jnp.sum(xr * xr, axis=-1, keepdims=True) * (1.0 / QK_ROPE) + EPS) * scale
        qr = inv * (xr * gr_ref[...] * tab[:, :128] + xw * gs_ref[...] * tab[:, 128:])
        q_ref[0, :, h * 256:h * 256 + 128] = qn.astype(q_ref.dtype)
        q_ref[0, :, h * 256 + 128:(h + 1) * 256] = qr.astype(q_ref.dtype)


def _q_up(cq, w_perm, tab, g_q, g_qa, g_r, g_s):
    B, T, _ = cq.shape
    hg = HEAD_GROUP
    tm = _pick_tile(T, 512, 8)
    scale = float((QK_NOPE + QK_ROPE) ** -0.5 * np.log2(np.e))
    return pl.pallas_call(
        functools.partial(_q_up_body, hg=hg, scale=scale),
        out_shape=jax.ShapeDtypeStruct((B, T, MLA_HEADS * 256), BF16),
        grid=(B, T // tm, MLA_HEADS // hg),
        in_specs=[pl.BlockSpec((1, tm, Q_LORA), lambda b, i, j: (b, i, 0)),
                  pl.BlockSpec((Q_LORA, hg * 384), lambda b, i, j: (0, j)),
                  pl.BlockSpec((tm, 256), lambda b, i, j: (i, 0)),
                  pl.BlockSpec((1, 128), lambda b, i, j: (0, 0)),
                  pl.BlockSpec((1, Q_LORA), lambda b, i, j: (0, 0)),
                  pl.BlockSpec((1, 128), lambda b, i, j: (0, 0)),
                  pl.BlockSpec((1, 128), lambda b, i, j: (0, 0))],
        out_specs=pl.BlockSpec((1, tm, hg * 256), lambda b, i, j: (b, i, j)),
        scratch_shapes=[pltpu.VMEM((tm, Q_LORA), BF16)],
        compiler_params=_cparams(3), name="q_up",
    )(cq, w_perm, tab, g_q, g_qa, g_r, g_s)


def _attn_body(q_ref, k_ref, v_ref, o_ref, *, n_split):
    k, v = k_ref[0], v_ref[0]
    rows = q_ref.shape[1] // n_split
    sl = [slice(i * rows, (i + 1) * rows) for i in range(n_split)]
    s = [lax.dot_general(q_ref[0, r, :], k, (((1,), (1,)), ((), ())), preferred_element_type=F32) for r in sl]
    m = [jnp.max(x, axis=-1, keepdims=True) for x in s]
    p = [jnp.exp2(x - mx) for x, mx in zip(s, m)]
    l = [jnp.sum(x, axis=-1, keepdims=True) for x in p]
    o = [_bdot(x.astype(BF16), v) for x in p]
    for r, ox, lx in zip(sl, o, l):
        o_ref[0, r, :] = (ox / lx).astype(o_ref.dtype)


def _attention(q, k, v):
    B, T, _ = q.shape
    S = k.shape[1]
    tq = _pick_tile(T, ATTN_ROWS, 8)
    return pl.pallas_call(
        functools.partial(_attn_body, n_split=ATTN_SPLIT if tq % (8 * ATTN_SPLIT) == 0 else 1),
        out_shape=jax.ShapeDtypeStruct((B, T, MLA_HEADS * V_HEAD), BF16),
        grid=(B, MLA_HEADS, T // tq),
        in_specs=[pl.BlockSpec((1, tq, 256), lambda b, h, i: (b, i, h)),
                  pl.BlockSpec((1, S, 256), lambda b, h, i: (b, 0, h)),
                  pl.BlockSpec((1, S, V_HEAD), lambda b, h, i: (b, 0, h))],
        out_specs=pl.BlockSpec((1, tq, V_HEAD), lambda b, h, i: (b, i, h)),
        compiler_params=_cparams(3), name="mla_attention",
    )(q, k, v)


def _conv_body(prev_ref, cur_ref, next_ref, w_ref, o_ref, buf_ref, *, n_lat, n_tiles, l2, scale, tr):
    j = pl.program_id(1)
    tc = cur_ref.shape[2]
    prev_ok = jnp.logical_and(j != 0, j != n_lat)
    next_ok = jnp.logical_and(j != n_lat - 1, j != n_tiles - 1)
    zeros8 = jnp.zeros((8, tc), F32)
    buf_ref[0:8, :] = jnp.where(prev_ok, prev_ref[0].astype(F32)[CONV_HALO - 8:], zeros8)
    buf_ref[8:8 + tr, :] = cur_ref[0].astype(F32)
    buf_ref[8 + tr:16 + tr, :] = jnp.where(next_ok, next_ref[0].astype(F32)[:8], zeros8)
    w = w_ref[...]
    y = jnp.zeros((tr, tc), F32)
    for t in range(CONV_W):
        r0 = 8 + t - CONV_W // 2
        y = y + buf_ref[r0:r0 + tr, :] * w[t:t + 1, :]
    y = _silu(y)
    if l2:
        for h in range(tc // 128):
            yh = y[:, h * 128:(h + 1) * 128]
            yh = yh * lax.rsqrt(jnp.sum(yh * yh, axis=-1, keepdims=True) + EPS) * scale
            o_ref[0, :, h * 128:(h + 1) * 128] = yh.astype(o_ref.dtype)
    else:
        o_ref[0] = y.astype(o_ref.dtype)


def _short_conv(x, col0, width, w8, *, n_lat_rows, l2, scale=1.0):
    B, R, _ = x.shape
    tr = _pick_tile(int(np.gcd(n_lat_rows, R - n_lat_rows)) if R > n_lat_rows else n_lat_rows, 256, CONV_HALO)
    tc = _pick_tile(width, 1024)
    n_tiles, n_lat = R // tr, n_lat_rows // tr
    cb = col0 // tc
    hpt = tr // CONV_HALO
    n_halo = R // CONV_HALO
    assert col0 % tc == 0
    return pl.pallas_call(
        functools.partial(_conv_body, n_lat=n_lat, n_tiles=n_tiles, l2=l2, scale=scale, tr=tr),
        out_shape=jax.ShapeDtypeStruct((B, R, width), BF16),
        grid=(B, n_tiles, width // tc),
        in_specs=[pl.BlockSpec((1, CONV_HALO, tc), lambda b, j, c: (b, jnp.maximum(j * hpt - 1, 0), cb + c)),
                  pl.BlockSpec((1, tr, tc), lambda b, j, c: (b, j, cb + c)),
                  pl.BlockSpec((1, CONV_HALO, tc),
                               lambda b, j, c: (b, jnp.minimum((j + 1) * hpt, n_halo - 1), cb + c)),
                  pl.BlockSpec((8, tc), lambda b, j, c: (0, c))],
        out_specs=pl.BlockSpec((1, tr, tc), lambda b, j, c: (b, j, c)),
        scratch_shapes=[pltpu.VMEM((tr + 16, tc), F32)],
        compiler_params=_cparams(3), name="short_conv",
    )(x, x, x, w8)


def _dn_consts():
    C = DN_CHUNK
    r = np.arange(C)[:, None]
    c = np.arange(C)[None, :]
    tri = np.stack([(c <= r), (c >= r)]).astype(np.float32)
    ones = np.ones((C, C), np.float32)
    rhs = np.stack([np.concatenate([(r > c).astype(np.float32), ones], 1),
                    np.concatenate([(r < c).astype(np.float32), ones], 1)])
    rhs3 = np.concatenate([rhs, rhs, rhs], axis=1)
    incl = np.stack([(r >= c), (r <= c)]).astype(np.float32)
    strict = np.stack([(r > c), (r < c)]).astype(np.float32)
    levels = []
    s = 1
    while s < C:
        levels.append(((r // (2 * s) == c // (2 * s)) & (r // s != c // s)).astype(np.float32))
        s *= 2
    return (jnp.asarray(tri, BF16), jnp.asarray(rhs3, BF16), jnp.asarray(incl), jnp.asarray(strict),
            jnp.asarray(np.stack(levels)))


def _dn_chunks(ch, consts, with_q):
    C = DN_CHUNK
    tri_ref, rhs3_ref, incl_ref, strict_ref, lev_ref = consts
    n = range(len(ch))
    d = [c["d"] for c in ch]
    last = [C - 1 if di == 0 else 0 for di in d]
    la = [-c["a_scale"] * jax.nn.softplus(c["a_row"] + c["dt_b"]) for c in ch]
    beta = [jax.nn.sigmoid(c["b_row"]) for c in ch]
    dg = []
    for i in n:
        hi = la[i].astype(BF16)
        r1 = la[i] - hi.astype(F32)
        mid = r1.astype(BF16)
        lo = (r1 - mid.astype(F32)).astype(BF16)
        t = tri_ref[d[i]]
        dg.append(_bdot(jnp.concatenate([t * hi, t * mid, t * lo], axis=1), rhs3_ref[d[i]]))
    diff = [g[:, :C] for g in dg]
    gcol = [g[:, C:] for g in dg]
    tail_row = [jnp.exp(diff[i][last[i]:last[i] + 1, :]) for i in n]
    g_last = [jnp.exp(gcol[i][last[i]:last[i] + 1, :]) for i in n]
    dec = [incl_ref[d[i]] * jnp.exp(diff[i] * incl_ref[d[i]]) for i in n]
    egc = [jnp.exp(g) for g in gcol]
    gram = [_bdot(c["k"], c["kT"]) for c in ch]
    nd = [strict_ref[d[i]] * gram[i] * dec[i] * beta[i] for i in n]
    eye = incl_ref[0] * incl_ref[1]
    x = [eye - nd[i] * lev_ref[0] for i in n]
    for lv in range(1, lev_ref.shape[0]):
        xb = [xi.astype(BF16) for xi in x]
        m = [(nd[i] * lev_ref[lv]).astype(BF16) for i in n]
        xm = [_bdot(xb[i], m[i]).astype(BF16) for i in n]
        x = [x[i] - _bdot(xm[i], xb[i]) for i in n]
    ke = [(ch[i]["k"].astype(F32) * egc[i]).astype(BF16) for i in n]
    y = [_bdot(x[i].astype(BF16), jnp.concatenate([ch[i]["v"], ke[i]], axis=1)) for i in n]
    s_b = [c["S"].astype(BF16) for c in ch]
    if with_q:
        qg = [(ch[i]["q"].astype(F32) * egc[i]).astype(BF16) for i in n]
        r2 = [_bdot(jnp.concatenate([y[i][:, DN_DV:].astype(BF16), qg[i]], axis=0), s_b[i]) for i in n]
        yks = [r[:C] for r in r2]
        qs = [r[C:] for r in r2]
    else:
        yks = [_bdot(y[i][:, DN_DV:].astype(BF16), s_b[i]) for i in n]
    vt = [(y[i][:, :DN_DV] - yks[i]).astype(BF16) for i in n]
    kt_tail = [(ch[i]["kT"].astype(F32) * (beta[i] * tail_row[i])).astype(BF16) for i in n]
    s_new = [ch[i]["S"] * g_last[i] + _bdot(kt_tail[i], vt[i]) for i in n]
    if not with_q:
        return s_new, [None for _ in n]
    qk = [_bdot(c["q"], c["kT"]) for c in ch]
    p = [(qk[i] * dec[i] * beta[i]).astype(BF16) for i in n]
    o = [qs[i] + _bdot(p[i], vt[i]) for i in n]
    return s_new, o


def _dn_body(alog_ref, dtb_ref, k_ref, kT_ref, v_ref, q_ref, ab_ref, z_ref, g_ref,
             tri_ref, rhs3_ref, incl_ref, strict_ref, lev_ref, o_ref, acc_ref, s_ref, *, hb, n_lat, n_ctx):
    C = DN_CHUNK
    hg = pl.program_id(1)
    consts = (tri_ref, rhs3_ref, incl_ref, strict_ref, lev_ref)
    acc_ref[...] = jnp.zeros(acc_ref.shape, F32)
    s_ref[...] = jnp.zeros(s_ref.shape, F32)

    def run_chunk(cf, cb, with_q):
        ch, where = [], []
        for h in range(hb):
            head = hg * hb + h
            for d, c in ((0, cf), (1, cb)):
                row0 = pl.multiple_of(c * C, C)
                cols = slice(h * 128, (h + 1) * 128)
                ch.append(dict(
                    k=k_ref[0, pl.ds(row0, C), cols], v=v_ref[0, pl.ds(row0, C), cols], kT=kT_ref[0, h, c],
                    q=q_ref[0, pl.ds(row0, C), cols] if with_q else None,
                    a_row=ab_ref[0, h, d, pl.ds(c, 1), :], b_row=ab_ref[0, h, 2 + d, pl.ds(c, 1), :],
                    a_scale=jnp.exp(alog_ref[d, head]), dt_b=dtb_ref[d, head], S=s_ref[2 * h + d], d=d))
                where.append((row0, cols))
        s_new, o = _dn_chunks(ch, consts, with_q)
        for i, (row0, cols) in enumerate(where):
            s_ref[i] = s_new[i]
            if with_q:
                acc_ref[pl.ds(row0, C), cols] += o[i]

    @pl.loop(0, n_ctx)
    def _(n):
        run_chunk(n_lat + n, n_lat + n_ctx - 1 - n, False)

    @pl.loop(0, n_lat)
    def _(n):
        run_chunk(n, n_lat - 1 - n, True)

    rows = 256 if (n_lat * C) % 256 == 0 else C

    @pl.loop(0, (n_lat * C) // rows)
    def _(i):
        r0 = pl.multiple_of(i * rows, rows)
        for h in range(hb):
            cols = slice(h * 128, (h + 1) * 128)
            od = acc_ref[pl.ds(r0, rows), cols]
            zz = z_ref[0, pl.ds(r0, rows), cols].astype(F32)
            o_ref[0, pl.ds(r0, rows), cols] = (_rms(od, g_ref[...]) * zz).astype(o_ref.dtype)


def _deltanet(k, kT, v, q, ab, z, a_log, dt_bias, g_dn):
    B, S, _ = k.shape
    T = q.shape[1]
    C = DN_CHUNK
    assert C == DN_DK == DN_DV == V7X_LANES
    hb = DN_HEADS_PER_STEP
    n_lat, n_ctx = T // C, (S - T) // C
    consts = list(_dn_consts())
    smem = pl.BlockSpec(memory_space=pltpu.SMEM)
    one = pl.Buffered(1)

    def full(a):
        nd = a.ndim
        return pl.BlockSpec(a.shape, lambda b, g, _n=nd: (0,) * _n)

    return pl.pallas_call(
        functools.partial(_dn_body, hb=hb, n_lat=n_lat, n_ctx=n_ctx),
        out_shape=jax.ShapeDtypeStruct((B, T, DN_V), BF16),
        grid=(B, DN_HEADS // hb),
        in_specs=[smem, smem,
                  pl.BlockSpec((1, S, hb * 128), lambda b, g: (b, 0, g), pipeline_mode=one),
                  pl.BlockSpec((1, hb, S // C, 128, C), lambda b, g: (b, g, 0, 0, 0), pipeline_mode=one),
                  pl.BlockSpec((1, S, hb * 128), lambda b, g: (b, 0, g), pipeline_mode=one),
                  pl.BlockSpec((1, T, hb * 128), lambda b, g: (b, 0, g), pipeline_mode=one),
                  pl.BlockSpec((1, hb, 4, S // C, C), lambda b, g: (b, g, 0, 0, 0)),
                  pl.BlockSpec((1, T, hb * 128), lambda b, g: (b, 0, g), pipeline_mode=one),
                  pl.BlockSpec((1, 128), lambda b, g: (0, 0))] + [full(a) for a in consts],
        out_specs=pl.BlockSpec((1, T, hb * 128), lambda b, g: (b, 0, g)),
        scratch_shapes=[pltpu.VMEM((T, hb * 128), F32), pltpu.VMEM((2 * hb, DN_DK, DN_DV), F32)],
        compiler_params=_cparams(2), name="deltanet",
    )(a_log, dt_bias, k, kT, v, q, ab, z, g_dn, *consts)


def _merge_body(a_ref, b_ref, wa_ref, wb_ref, ga_ref, gb_ref, o_ref):
    oa = _bdot(a_ref[0], wa_ref[...])
    ob = _bdot(b_ref[0], wb_ref[...])
    o_ref[0] = (ga_ref[0].astype(F32) * oa + gb_ref[0].astype(F32) * ob).astype(o_ref.dtype)


def _merge(att, ob, w_oa, w_ob, gates):
    B, T, Ka = att.shape
    D = w_oa.shape[1]
    tm = _pick_tile(T, 1024, 8)
    tn = _pick_tile(D, 512)
    nj = D // tn
    return pl.pallas_call(
        _merge_body,
        out_shape=jax.ShapeDtypeStruct((B, T, D), BF16),
        grid=(B, T // tm, nj),
        in_specs=[pl.BlockSpec((1, tm, Ka), lambda b, i, j: (b, i, 0)),
                  pl.BlockSpec((1, tm, Ka), lambda b, i, j: (b, i, 0)),
                  pl.BlockSpec((Ka, tn), lambda b, i, j: (0, j)),
                  pl.BlockSpec((Ka, tn), lambda b, i, j: (0, j)),
                  pl.BlockSpec((1, tm, tn), lambda b, i, j: (b, i, j)),
                  pl.BlockSpec((1, tm, tn), lambda b, i, j: (b, i, j + nj))],
        out_specs=pl.BlockSpec((1, tm, tn), lambda b, i, j: (b, i, j)),
        compiler_params=_cparams(3), name="merge",
    )(att, ob, w_oa, w_ob, gates, gates)


def _norm_router_body(x_ref, g_ref, sh_ref, sc_ref, whl_ref, br_ref, h_ref, id_ref, gate_ref):
    h = _rms(x_ref[0], g_ref[...]) * (1.0 + sc_ref[0]) + sh_ref[0]
    half = h.shape[1] // 2
    h_ref[0] = _pack_bf16_pair(h[:, :half], h[:, half:])
    hh = h.astype(BF16)
    hl = (h - hh.astype(F32)).astype(BF16)
    r = _bdot(hh, whl_ref[...])
    logits = r[:, :128] + r[:, 128:] + _bdot(hl, whl_ref[:, :128]) + br_ref[...]
    lane = lax.broadcasted_iota(jnp.int32, logits.shape, 1)
    neg = jnp.float32(-jnp.inf)
    big = jnp.int32(1 << 20)
    is_g = lane < N_GROUPS
    lg = jnp.where(is_g, logits, neg)
    mg = jnp.max(lg, axis=-1, keepdims=True)
    eg = jnp.where(is_g, jnp.exp(lg - mg), 0.0)
    pg = eg / jnp.sum(eg, axis=-1, keepdims=True)
    pg_top = jnp.max(pg, axis=-1, keepdims=True)
    g_top = jnp.min(jnp.where(jnp.logical_and(is_g, pg == pg_top), lane, big), axis=-1, keepdims=True)
    lo = N_GROUPS + EXPERTS_PER_GROUP * g_top
    is_e = jnp.logical_and(lane >= lo, lane < lo + EXPERTS_PER_GROUP)
    le = jnp.where(is_e, logits, neg)
    me = jnp.max(le, axis=-1, keepdims=True)
    ee = jnp.where(is_e, jnp.exp(le - me), 0.0)
    pe = ee / jnp.sum(ee, axis=-1, keepdims=True)
    p1 = jnp.max(pe, axis=-1, keepdims=True)
    i1 = jnp.min(jnp.where(jnp.logical_and(is_e, pe == p1), lane, big), axis=-1, keepdims=True)
    rest = jnp.logical_and(is_e, lane != i1)
    pe2 = jnp.where(rest, pe, -1.0)
    p2 = jnp.max(pe2, axis=-1, keepdims=True)
    i2 = jnp.min(jnp.where(jnp.logical_and(rest, pe2 == p2), lane, big), axis=-1, keepdims=True)
    den = p1 + p2
    id_ref[0] = jnp.where(lane == 0, i1 - N_GROUPS, jnp.where(lane == 1, i2 - N_GROUPS, 0))
    gate_ref[0] = jnp.where(lane == 0, p1 / den * pg_top, jnp.where(lane == 1, p2 / den * pg_top, 0.0))


def _norm_router(x, g, shift, scale, w_hl, b_r):
    B, T, D = x.shape
    tr = _pick_tile(T, 256, 8)
    return pl.pallas_call(
        _norm_router_body,
        out_shape=(jax.ShapeDtypeStruct((B, T, D // 2), jnp.uint32),
                   jax.ShapeDtypeStruct((B, T, 128), jnp.int32),
                   jax.ShapeDtypeStruct((B, T, 128), F32)),
        grid=(B, T // tr),
        in_specs=[pl.BlockSpec((1, tr, D), lambda b, i: (b, i, 0)),
                  pl.BlockSpec((1, D), lambda b, i: (0, 0)),
                  pl.BlockSpec((1, 1, D), lambda b, i: (b, 0, 0)),
                  pl.BlockSpec((1, 1, D), lambda b, i: (b, 0, 0)),
                  pl.BlockSpec((D, 256), lambda b, i: (0, 0)),
                  pl.BlockSpec((1, 128), lambda b, i: (0, 0))],
        out_specs=(pl.BlockSpec((1, tr, D // 2), lambda b, i: (b, i, 0)),
                   pl.BlockSpec((1, tr, 128), lambda b, i: (b, i, 0)),
                   pl.BlockSpec((1, tr, 128), lambda b, i: (b, i, 0))),
        compiler_params=_cparams(2), name="norm_router",
    )(x, g, shift, scale, w_hl, b_r)


def _row_gather(idx_ref, src_ref, buf_ref, sem, slot, n_groups):
    def row_copy(r, s):
        return pltpu.make_async_copy(src_ref.at[pl.ds(s, 1)], buf_ref.at[slot, pl.ds(r, 1)], sem.at[slot])

    def start():
        def body(g, c):
            for j in range(ROW_DMA_GROUP):
                r = g * ROW_DMA_GROUP + j
                row_copy(r, idx_ref[0, 0, r]).start(priority=j % 2)
            return c
        lax.fori_loop(0, n_groups, body, 0)

    def wait():
        def body(g, c):
            for j in range(ROW_DMA_GROUP):
                row_copy(g * ROW_DMA_GROUP + j, 0).wait()
            return c
        lax.fori_loop(0, n_groups, body, 0)

    return start, wait


def _groups(n_rows):
    return (n_rows + ROW_DMA_GROUP - 1) // ROW_DMA_GROUP


def _moe_gather_body(nv_ref, src_ref, nxt_ref, h_ref, o_ref, buf_ref, sem, *, nb):
    i = pl.program_id(0)
    slot = lax.rem(i, 2)
    nv = nv_ref[i]
    nv_next = jnp.where(i + 1 < nb, nv_ref[jnp.minimum(i + 1, nb - 1)], 0)
    start_cur, wait_cur = _row_gather(src_ref, h_ref, buf_ref, sem, slot, _groups(nv))
    start_nxt, _ = _row_gather(nxt_ref, h_ref, buf_ref, sem, 1 - slot, _groups(nv_next))

    @pl.when(i == 0)
    def _():
        buf_ref[...] = jnp.zeros(buf_ref.shape, buf_ref.dtype)
        start_cur()

    start_nxt()

    @pl.when(nv > 0)
    def _():
        wait_cur()
        lo, hi = _unpack_bf16_pair(buf_ref[slot])
        half = lo.shape[1]
        o_ref[:, :half] = lo.astype(o_ref.dtype)
        o_ref[:, half:] = hi.astype(o_ref.dtype)

    @pl.when(nv == 0)
    def _():
        o_ref[...] = jnp.zeros(o_ref.shape, o_ref.dtype)


def _moe_gather(h, src, n_valid):
    n, D = h.shape[0], 2 * h.shape[1]
    R = src.shape[0]
    nb = R // MOE_ROWS
    src3 = src.reshape(nb, 1, MOE_ROWS)
    return pl.pallas_call(
        functools.partial(_moe_gather_body, nb=nb),
        out_shape=jax.ShapeDtypeStruct((R, D), BF16),
        grid_spec=pltpu.PrefetchScalarGridSpec(
            num_scalar_prefetch=1, grid=(nb,),
            in_specs=[pl.BlockSpec((1, 1, MOE_ROWS), lambda i, nv: (i, 0, 0), memory_space=pltpu.SMEM),
                      pl.BlockSpec((1, 1, MOE_ROWS), lambda i, nv: (jnp.minimum(i + 1, nb - 1), 0, 0),
                                   memory_space=pltpu.SMEM),
                      pl.BlockSpec(memory_space=pl.ANY)],
            out_specs=pl.BlockSpec((MOE_ROWS, D), lambda i, nv: (i, 0)),
            scratch_shapes=[pltpu.VMEM((2, MOE_ROWS, D // 2), jnp.uint32), pltpu.SemaphoreType.DMA((2,))]),
        compiler_params=_cparams(1), name="moe_gather",
    )(n_valid, src3, src3, h)


def _moe_halves(nv, o_ref, compute):
    half = MOE_ROWS // 2
    for r0 in (0, half):
        rows = slice(r0, r0 + half)

        @pl.when(nv > r0)
        def _():
            compute(rows)

        @pl.when(nv <= r0)
        def _():
            o_ref[rows, :] = jnp.zeros((half, o_ref.shape[1]), o_ref.dtype)


def _moe_up_body(be_ref, nv_ref, x_ref, w1_ref, w3_ref, o_ref, w1b_ref, w3b_ref):
    i = pl.program_id(1)
    changed = jnp.logical_or(i == 0, be_ref[i] != be_ref[jnp.maximum(i - 1, 0)])

    @pl.when(changed)
    def _():
        w1b_ref[...] = w1_ref[0].astype(BF16)
        w3b_ref[...] = w3_ref[0].astype(BF16)

    def compute(rows):
        x = x_ref[rows, :]
        a = _bdot(x, w1b_ref[...])
        b = _bdot(x, w3b_ref[...])
        o_ref[rows, :] = (_silu(a) * b).astype(o_ref.dtype)

    _moe_halves(nv_ref[i], o_ref, compute)


def _moe_down_body(be_ref, nv_ref, h_ref, w2_ref, o_ref, w2b_ref):
    i = pl.program_id(1)
    changed = jnp.logical_or(i == 0, be_ref[i] != be_ref[jnp.maximum(i - 1, 0)])

    @pl.when(changed)
    def _():
        w2b_ref[...] = w2_ref[0].astype(BF16)

    def compute(rows):
        y = _bdot(h_ref[rows, :], w2b_ref[...])
        half = y.shape[1] // 2
        o_ref[rows, :] = _pack_bf16_pair(y[:, :half], y[:, half:])

    _moe_halves(nv_ref[i], o_ref, compute)


def _moe_experts(x_pad, blk_expert, n_valid, w1, w3, w2):
    R, D = x_pad.shape
    nb = R // MOE_ROWS
    Hd = w1.shape[2]
    th = _pick_tile(Hd, 512)
    hbuf = pl.pallas_call(
        _moe_up_body,
        out_shape=jax.ShapeDtypeStruct((R, Hd), BF16),
        grid_spec=pltpu.PrefetchScalarGridSpec(
            num_scalar_prefetch=2, grid=(Hd // th, nb),
            in_specs=[pl.BlockSpec((MOE_ROWS, D), lambda j, i, be, nv: (i, 0)),
                      pl.BlockSpec((1, D, th), lambda j, i, be, nv: (be[i], 0, j)),
                      pl.BlockSpec((1, D, th), lambda j, i, be, nv: (be[i], 0, j))],
            out_specs=pl.BlockSpec((MOE_ROWS, th), lambda j, i, be, nv: (i, j)),
            scratch_shapes=[pltpu.VMEM((D, th), BF16), pltpu.VMEM((D, th), BF16)]),
        compiler_params=_cparams(2), name="moe_up",
    )(blk_expert, n_valid, x_pad, w1, w3)
    tn = _pick_tile(D, MOE_DOWN_COLS)
    return pl.pallas_call(
        _moe_down_body,
        out_shape=jax.ShapeDtypeStruct((R, D // 2), jnp.uint32),
        grid_spec=pltpu.PrefetchScalarGridSpec(
            num_scalar_prefetch=2, grid=(D // tn, nb),
            in_specs=[pl.BlockSpec((MOE_ROWS, Hd), lambda j, i, be, nv: (i, 0)),
                      pl.BlockSpec((1, Hd, tn), lambda j, i, be, nv: (be[i], 0, j))],
            out_specs=pl.BlockSpec((MOE_ROWS, tn // 2), lambda j, i, be, nv: (i, j)),
            scratch_shapes=[pltpu.VMEM((Hd, tn), BF16)]),
        compiler_params=_cparams(2), name="moe_down",
    )(blk_expert, n_valid, hbuf, w2)


def _combine_body(dest_ref, nxt_ref, x_ref, p_ref, g_ref, y_ref, o_ref, buf_ref, sem, *, tr, n_steps, tn):
    t = pl.program_id(0)
    slot = lax.rem(t, 2)
    start_cur, wait_cur = _row_gather(dest_ref, y_ref, buf_ref, sem, slot, TOP_K * tr // ROW_DMA_GROUP)
    start_nxt, _ = _row_gather(nxt_ref, y_ref, buf_ref, sem, 1 - slot, TOP_K * tr // ROW_DMA_GROUP)

    @pl.when(t == 0)
    def _():
        start_cur()

    @pl.when(t + 1 < n_steps)
    def _():
        start_nxt()

    wait_cur()
    p = p_ref[0]
    half = tn // 2
    for j in range(x_ref.shape[2] // tn):
        words = slice(j * half, (j + 1) * half)
        lo0, hi0 = _unpack_bf16_pair(buf_ref[slot, 0:tr, words])
        lo1, hi1 = _unpack_bf16_pair(buf_ref[slot, tr:2 * tr, words])
        for y0, y1, c0 in ((lo0, lo1, j * tn), (hi0, hi1, j * tn + half)):
            cols = slice(c0, c0 + half)
            ffn = p[:, 0:1] * y0 + p[:, 1:2] * y1
            o_ref[0, :, cols] = x_ref[0, :, cols] + g_ref[0, :, cols] * ffn


def _combine(x, y_pad, dest, gate, g2):
    B, T, D = x.shape
    tr = _pick_tile(T, 256, 8)
    nt = T // tr
    n_steps = B * nt
    dest_t = dest.reshape(n_steps, tr, TOP_K).transpose(0, 2, 1).reshape(n_steps, 1, TOP_K * tr)
    row = pl.BlockSpec((1, tr, D), lambda t: (t // nt, t % nt, 0))
    return pl.pallas_call(
        functools.partial(_combine_body, tr=tr, n_steps=n_steps, tn=_pick_tile(D, MOE_DOWN_COLS)),
        out_shape=jax.ShapeDtypeStruct((B, T, D), F32),
        grid=(n_steps,),
        in_specs=[pl.BlockSpec((1, 1, TOP_K * tr), lambda t: (t, 0, 0), memory_space=pltpu.SMEM),
                  pl.BlockSpec((1, 1, TOP_K * tr), lambda t: (jnp.minimum(t + 1, n_steps - 1), 0, 0),
                               memory_space=pltpu.SMEM),
                  row,
                  pl.BlockSpec((1, tr, 128), lambda t: (t // nt, t % nt, 0)),
                  pl.BlockSpec((1, 1, D), lambda t: (t // nt, 0, 0)),
                  pl.BlockSpec(memory_space=pl.ANY)],
        out_specs=row,
        scratch_shapes=[pltpu.VMEM((2, TOP_K * tr, D // 2), jnp.uint32), pltpu.SemaphoreType.DMA((2,))],
        compiler_params=_cparams(1), name="moe_combine",
    )(dest_t, dest_t, x, gate, g2, y_pad)


def _rope_tables(T, Tc):
    rows = T // GRID_W
    row = jnp.repeat(jnp.arange(rows), GRID_W).astype(F32)
    col = jnp.tile(jnp.arange(GRID_W), rows).astype(F32)
    half = QK_ROPE // 2
    inv_freq = ROPE_THETA ** (-jnp.arange(0, half, 2, dtype=F32) / half)
    ang = jnp.concatenate([row[:, None] * inv_freq, col[:, None] * inv_freq], axis=-1)
    cos, sin = jnp.cos(ang), jnp.sin(ang)
    z64 = jnp.zeros((T, 64), F32)
    lat = jnp.concatenate([cos, cos, z64, -sin, sin, z64], axis=-1)
    ctx = jnp.concatenate([jnp.ones((Tc, 64), F32), jnp.zeros((Tc, 192), F32)], axis=-1)
    return jnp.concatenate([lat, ctx], axis=0)


_PERM = np.concatenate([np.arange(0, QK_ROPE, 2), np.arange(1, QK_ROPE, 2)])
_PERM_SW = np.concatenate([np.arange(1, QK_ROPE, 2), np.arange(0, QK_ROPE, 2)])


def _pad_cols(a, n):
    return jnp.pad(a, ((0, 0), (0, n - a.shape[1])))


def _moe_dispatch(expert_ids, n_tokens):
    a = n_tokens * TOP_K
    e_flat = expert_ids.reshape(a)
    onehot = (e_flat[:, None] == jnp.arange(N_EXPERTS)[None, :]).astype(F32)
    ch = 128 if a % 128 == 0 else a
    nc = a // ch
    hp = lax.Precision.HIGHEST
    within = jnp.einsum("ij,cje->cie", jnp.tril(jnp.ones((ch, ch), F32)), onehot.reshape(nc, ch, N_EXPERTS),
                        precision=hp)
    before = jnp.einsum("ij,je->ie", jnp.tril(jnp.ones((nc, nc), F32), -1), within[:, -1, :], precision=hp)
    csum = (within + before[:, None, :]).reshape(a, N_EXPERTS)
    rank = jnp.sum(csum * onehot, axis=1).astype(jnp.int32) - 1
    counts = csum[-1].astype(jnp.int32)
    padded = (counts + MOE_ROWS - 1) // MOE_ROWS * MOE_ROWS
    pad_end = jnp.cumsum(padded)
    pad_start = pad_end - padded
    dest = pad_start[e_flat] + rank
    n_blocks = -(-a // MOE_ROWS) + N_EXPERTS
    src = jnp.zeros((n_blocks * MOE_ROWS,), jnp.int32).at[dest].set(jnp.arange(a, dtype=jnp.int32) // TOP_K)
    blk_expert = jnp.minimum(
        jnp.searchsorted(pad_end, jnp.arange(n_blocks) * MOE_ROWS, side="right"), N_EXPERTS - 1).astype(jnp.int32)
    n_valid = jnp.clip(counts[blk_expert] - (jnp.arange(n_blocks) * MOE_ROWS - pad_start[blk_expert]), 0, MOE_ROWS)
    return dest, src, blk_expert, n_valid.astype(jnp.int32)


def _layer(x, c, ctx, c_ctx, p):
    B, T, D = x.shape
    Tc = ctx.shape[1]
    S = T + Tc
    H = MLA_HEADS

    cin = jnp.zeros((1, 16, D), F32).at[0, :B].set(c).at[0, B].set(c_ctx)
    mod = _matmul(cin, p["w_mod"], tm=16, tn=1024, out_dtype=F32, pre=_silu,
                  epi=lambda acc, b: acc + b, extras=[(p["b_mod"].reshape(1, 1, -1), "bcol")], name="mod")[0]
    mod_l = mod[:B].reshape(B, N_MOD, 1, D)
    mod_c = mod[B].reshape(N_MOD, 1, 1, D)

    h = _norm_mod(x, ctx, p["norm1_g"].reshape(1, D), mod_l[:, 0], mod_l[:, 1], mod_c[0], mod_c[1])

    w_in = p["w_in"]
    wkr = w_in[:, OFF_KR:OFF_AB]
    w_small = jnp.concatenate([w_in[:, OFF_CKV:OFF_KR], _pad_cols(wkr[:, _PERM], 128),
                               _pad_cols(wkr[:, _PERM_SW], 128), _pad_cols(w_in[:, OFF_AB:OFF_DK], 128)],
                              axis=1).astype(BF16)
    p_small = _matmul(h, w_small, tm=1152, tn=896, out_dtype=F32, name="in_small")
    p_dkv = _matmul(h, w_in[:, OFF_DK:OFF_DQ].astype(BF16), tm=1152, name="in_dkv")
    p_dq = _matmul(h, w_in[:, OFF_DQ:OFF_CQ].astype(BF16), rows=T, name="in_dq")
    p_cq = _matmul(h, w_in[:, OFF_CQ:OFF_Z].astype(BF16), rows=T, out_dtype=F32, name="in_cq")
    z_act = _matmul(h, w_in[:, OFF_Z:OFF_GATE].astype(BF16), rows=T, epi=_silu, name="in_z")
    gates = _matmul(h, w_in[:, OFF_GATE:].astype(BF16), rows=T, epi=jax.nn.sigmoid, name="in_gate")

    tab = _rope_tables(T, Tc)
    gkr = p["k_rope_norm_g"]
    ckv_n, k_rope = _mla_small(p_small, tab, p["kv_norm_g"].reshape(1, -1),
                               _pad_cols(gkr[_PERM][None], 128), _pad_cols(gkr[_PERM_SW][None], 128))
    w_ukv = p["w_ukv"].reshape(KV_LORA, H, QK_NOPE + V_HEAD)
    hg = HEAD_GROUP
    w_kv_perm = jnp.concatenate(
        [w_ukv[:, :, :QK_NOPE].reshape(KV_LORA, H // hg, hg * QK_NOPE),
         w_ukv[:, :, QK_NOPE:].reshape(KV_LORA, H // hg, hg * V_HEAD)], axis=2).reshape(KV_LORA, H * 256).astype(BF16)
    k_all, v_all = _kv_up(ckv_n, w_kv_perm, k_rope, p["k_norm_g"].reshape(1, -1))

    w_uq = p["w_uq"].reshape(Q_LORA, H, QK_NOPE + QK_ROPE)
    wq_r = w_uq[:, :, QK_NOPE:]
    zpad = jnp.zeros((Q_LORA, H, 64), F32)
    w_q_perm = jnp.concatenate(
        [w_uq[:, :, :QK_NOPE].reshape(Q_LORA, H // hg, hg * 128),
         jnp.concatenate([wq_r[:, :, _PERM], zpad], axis=2).reshape(Q_LORA, H // hg, hg * 128),
         jnp.concatenate([wq_r[:, :, _PERM_SW], zpad], axis=2).reshape(Q_LORA, H // hg, hg * 128)],
        axis=2).reshape(Q_LORA, H * 384).astype(BF16)
    gqr = p["q_rope_norm_g"]
    q_all = _q_up(p_cq, w_q_perm, tab[:T], p["q_norm_g"].reshape(1, -1), p["q_a_norm_g"].reshape(1, -1),
                  _pad_cols(gqr[_PERM][None], 128), _pad_cols(gqr[_PERM_SW][None], 128))
    att = _attention(q_all, k_all, v_all)

    cw = jnp.pad(p["conv_w"], ((0, 8 - CONV_W), (0, 0)))
    dn_k = _short_conv(p_dkv, 0, DN_QK, cw[:, :DN_QK], n_lat_rows=T, l2=True)
    dn_v = _short_conv(p_dkv, DN_QK, DN_V, cw[:, DN_QK:DN_QK + DN_V], n_lat_rows=T, l2=False)
    dn_q = _short_conv(p_dq, 0, DN_QK, cw[:, DN_QK + DN_V:], n_lat_rows=T, l2=True, scale=float(DN_DK ** -0.5))
    C = DN_CHUNK
    dn_kT = dn_k.reshape(B, S // C, C, DN_HEADS, DN_DK).transpose(0, 3, 1, 4, 2)
    ab = p_small[:, :, 768:768 + 4 * DN_HEADS].reshape(B, S // C, C, 4, DN_HEADS).transpose(0, 4, 3, 1, 2)
    ob_in = _deltanet(dn_k, dn_kT, dn_v, dn_q, ab, z_act, p["a_log"], p["dt_bias"], p["dn_norm_g"].reshape(1, -1))

    mix = _merge(att, ob_in, p["w_oa"].astype(BF16), p["w_ob"].astype(BF16), gates)
    xl = _matmul(mix, p["w_out"].astype(BF16), out_dtype=F32,
                 epi=lambda acc, xr, g: xr + g * acc, extras=[(x, "tile"), (mod_l[:, 2], "bcol")], name="out_proj")

    w_r = _pad_cols(jnp.concatenate([p["w_rg"], p["w_re"]], axis=1), 128)
    w_r_hi = w_r.astype(BF16)
    w_r_lo = (w_r - w_r_hi.astype(F32)).astype(BF16)
    b_r = _pad_cols(jnp.concatenate([p["b_rg"], p["b_re"]])[None], 128)
    h2, ids, gate = _norm_router(xl, p["norm2_g"].reshape(1, D), mod_l[:, 3], mod_l[:, 4],
                                 jnp.concatenate([w_r_hi, w_r_lo], axis=1), b_r)
    n = B * T
    dest, src, blk_expert, n_valid = _moe_dispatch(ids[:, :, :TOP_K].reshape(n, TOP_K), n)
    x_pad = _moe_gather(h2.reshape(n, D // 2), src, n_valid)
    y_pad = _moe_experts(x_pad, blk_expert, n_valid, p["w1"], p["w3"], p["w2"])
    return _combine(xl, y_pad, dest.reshape(n, TOP_K), gate, mod_l[:, 5])


def kernel(x, c, ctx, c_ctx, norm1_g, norm2_g, w_mod, b_mod, w_in, q_a_norm_g, w_uq, kv_norm_g, w_ukv, q_norm_g,
           q_rope_norm_g, k_norm_g, k_rope_norm_g, conv_w, a_log, dt_bias, dn_norm_g, w_oa, w_ob, w_out, w_rg, b_rg,
           w_re, b_re, w1, w3, w2):
    depth = norm1_g.shape[0]
    assert depth == 1, "single-layer block: the context stream is read, never updated"
    layer = 0
    p = {
        "norm1_g": norm1_g[layer], "norm2_g": norm2_g[layer], "w_mod": w_mod[layer], "b_mod": b_mod[layer],
        "w_in": w_in[layer], "q_a_norm_g": q_a_norm_g[layer], "w_uq": w_uq[layer], "kv_norm_g": kv_norm_g[layer],
        "w_ukv": w_ukv[layer], "q_norm_g": q_norm_g[layer], "q_rope_norm_g": q_rope_norm_g[layer],
        "k_norm_g": k_norm_g[layer], "k_rope_norm_g": k_rope_norm_g[layer], "conv_w": conv_w[layer],
        "a_log": a_log[layer], "dt_bias": dt_bias[layer], "dn_norm_g": dn_norm_g[layer], "w_oa": w_oa[layer],
        "w_ob": w_ob[layer], "w_out": w_out[layer], "w_rg": w_rg[layer], "b_rg": b_rg[layer], "w_re": w_re[layer],
        "b_re": b_re[layer], "w1": w1[layer], "w3": w3[layer], "w2": w2[layer],
    }
    return _layer(x, c, ctx, c_ctx, p)
```

```python
import functools

import numpy as np
import jax
import jax.numpy as jnp
from jax import lax
from jax.experimental import pallas as pl
from jax.experimental.pallas import tpu as pltpu

F32 = jnp.float32
BF16 = jnp.bfloat16

EPS = 1e-6
GRID_W = 64
N_MOD = 6
MLA_HEADS = 16
Q_LORA = 1024
KV_LORA = 512
QK_NOPE = 128
QK_ROPE = 64
V_HEAD = 128
ROPE_THETA = 10000.0
DN_HEADS = 16
DN_DK = 128
DN_DV = 128
DN_QK = DN_HEADS * DN_DK
DN_V = DN_HEADS * DN_DV
CONV_W = 5
N_GROUPS = 4
EXPERTS_PER_GROUP = 8
N_EXPERTS = N_GROUPS * EXPERTS_PER_GROUP
TOP_K = 2
EXPERT_HIDDEN = 1024

OFF_CKV = 0
OFF_KR = OFF_CKV + KV_LORA
OFF_AB = OFF_KR + QK_ROPE
OFF_DK = OFF_AB + 4 * DN_HEADS
OFF_DV = OFF_DK + DN_QK
OFF_DQ = OFF_DV + DN_V
OFF_CQ = OFF_DQ + DN_QK
OFF_Z = OFF_CQ + Q_LORA
OFF_GATE = OFF_Z + DN_V

V7X_LANES = 128
V7X_VMEM_BYTES = 64 * 1024 * 1024
VMEM_LIMIT = V7X_VMEM_BYTES - 8 * 1024 * 1024

DN_CHUNK = 128
MOE_ROWS = 512
MOE_DOWN_COLS = 2048
ROW_DMA_GROUP = 8
HEAD_GROUP = 4
DN_HEADS_PER_STEP = 8
ATTN_ROWS = 2048
ATTN_SPLIT = 8
CONV_HALO = 16


def _cparams(n_grid):
    return pltpu.CompilerParams(dimension_semantics=("arbitrary",) * n_grid, vmem_limit_bytes=VMEM_LIMIT)


def _pick_tile(n, target, mult=V7X_LANES):
    if n <= target:
        return n
    best = None
    t = mult
    while t <= target:
        if n % t == 0:
            best = t
        t += mult
    assert best is not None, (n, target)
    return best


def _rms(x, g):
    return x * lax.rsqrt(jnp.mean(x * x, axis=-1, keepdims=True) + EPS) * g


def _silu(x):
    return x * jax.nn.sigmoid(x)


def _bdot(a, b):
    return jnp.dot(a, b, preferred_element_type=F32)


def _pack_bf16_pair(lo, hi):
    lo_bits = lax.bitcast_convert_type(lo.astype(BF16).astype(F32), jnp.uint32)
    hi_bits = lax.bitcast_convert_type(hi.astype(BF16).astype(F32), jnp.uint32)
    return (lo_bits >> 16) | (hi_bits & jnp.uint32(0xFFFF0000))


def _unpack_bf16_pair(w):
    lo = lax.bitcast_convert_type(w << 16, F32)
    hi = lax.bitcast_convert_type(w & jnp.uint32(0xFFFF0000), F32)
    return lo, hi


def _mm_body(*refs, pre, epi, n_extra, cached):
    x_ref, w_ref = refs[0], refs[1]
    extra = refs[2:2 + n_extra]
    o_ref = refs[2 + n_extra]
    if cached:
        xs_ref = refs[3 + n_extra]

        @pl.when(pl.program_id(2) == 0)
        def _():
            xs_ref[...] = pre(x_ref[0]).astype(BF16)

        x = xs_ref[...]
    else:
        x = x_ref[0]
    w = w_ref[...]
    if w.dtype != BF16:
        w = w.astype(BF16)
    acc = _bdot(x, w)
    if epi is not None:
        acc = epi(acc, *[e[0] for e in extra])
    o_ref[0] = acc.astype(o_ref.dtype)


def _matmul(x, w, *, rows=None, tm=1024, tn=512, out_dtype=BF16, pre=None, epi=None, extras=(), name="mm"):
    B, S, K = x.shape
    N = w.shape[1]
    rows = S if rows is None else rows
    tm = _pick_tile(rows, tm)
    tn = _pick_tile(N, tn)
    grid = (B, rows // tm, N // tn)
    in_specs = [pl.BlockSpec((1, tm, K), lambda b, i, j: (b, i, 0)),
                pl.BlockSpec((K, tn), lambda b, i, j: (0, j))]
    args = [x, w]
    for arr, kind in extras:
        if kind == "tile":
            in_specs.append(pl.BlockSpec((1, tm, tn), lambda b, i, j: (b, i, j)))
        elif kind == "bcol":
            if arr.shape[0] == 1:
                in_specs.append(pl.BlockSpec((1, 1, tn), lambda b, i, j: (0, 0, j)))
            else:
                in_specs.append(pl.BlockSpec((1, 1, tn), lambda b, i, j: (b, 0, j)))
        else:
            raise ValueError(kind)
        args.append(arr)
    cached = pre is not None
    scratch = [pltpu.VMEM((tm, K), BF16)] if cached else []
    if not cached:
        assert x.dtype == BF16
    return pl.pallas_call(
        functools.partial(_mm_body, pre=pre, epi=epi, n_extra=len(extras), cached=cached),
        out_shape=jax.ShapeDtypeStruct((B, rows, N), out_dtype),
        grid=grid, in_specs=in_specs,
        out_specs=pl.BlockSpec((1, tm, tn), lambda b, i, j: (b, i, j)),
        scratch_shapes=scratch, compiler_params=_cparams(3), name=name,
    )(*args)


def _norm_mod_body(x_ref, c_ref, g_ref, sl_ref, scl_ref, sc_ref, scc_ref, o_ref, *, n_lat):
    j = pl.program_id(1)

    @pl.when(j < n_lat)
    def _():
        h = _rms(x_ref[0], g_ref[...])
        o_ref[0] = (h * (1.0 + scl_ref[0]) + sl_ref[0]).astype(o_ref.dtype)

    @pl.when(j >= n_lat)
    def _():
        h = _rms(c_ref[0], g_ref[...])
        o_ref[0] = (h * (1.0 + scc_ref[0]) + sc_ref[0]).astype(o_ref.dtype)


def _norm_mod(x, ctx, g, shift_l, scale_l, shift_c, scale_c):
    B, T, D = x.shape
    Tc = ctx.shape[1]
    tr = _pick_tile(int(np.gcd(T, Tc)), 256, 8)
    n_lat, n_ctx = T // tr, Tc // tr
    return pl.pallas_call(
        functools.partial(_norm_mod_body, n_lat=n_lat),
        out_shape=jax.ShapeDtypeStruct((B, T + Tc, D), BF16),
        grid=(B, n_lat + n_ctx),
        in_specs=[pl.BlockSpec((1, tr, D), lambda b, j: (b, jnp.minimum(j, n_lat - 1), 0)),
                  pl.BlockSpec((1, tr, D), lambda b, j: (b, jnp.maximum(j - n_lat, 0), 0)),
                  pl.BlockSpec((1, D), lambda b, j: (0, 0)),
                  pl.BlockSpec((1, 1, D), lambda b, j: (b, 0, 0)),
                  pl.BlockSpec((1, 1, D), lambda b, j: (b, 0, 0)),
                  pl.BlockSpec((1, 1, D), lambda b, j: (0, 0, 0)),
                  pl.BlockSpec((1, 1, D), lambda b, j: (0, 0, 0))],
        out_specs=pl.BlockSpec((1, tr, D), lambda b, j: (b, j, 0)),
        compiler_params=_cparams(2), name="norm_mod",
    )(x, ctx, g, shift_l, scale_l, shift_c, scale_c)


def _mla_small_body(p_ref, tab_ref, gkv_ref, gkr_ref, gks_ref, ckv_ref, kr_ref):
    p = p_ref[0]
    ckv_ref[0] = _rms(p[:, :KV_LORA], gkv_ref[...]).astype(ckv_ref.dtype)
    xr = p[:, KV_LORA:KV_LORA + 128]
    xs = p[:, KV_LORA + 128:KV_LORA + 256]
    inv = lax.rsqrt(jnp.sum(xr * xr, axis=-1, keepdims=True) * (1.0 / QK_ROPE) + EPS)
    tab = tab_ref[...]
    kr = inv * (xr * gkr_ref[...] * tab[:, :128] + xs * gks_ref[...] * tab[:, 128:])
    kr_ref[0] = kr.astype(kr_ref.dtype)


def _mla_small(p_small, tab, g_kv, g_kr, g_ks):
    B, S, W = p_small.shape
    tr = _pick_tile(S, 768, 8)
    return pl.pallas_call(
        _mla_small_body,
        out_shape=(jax.ShapeDtypeStruct((B, S, KV_LORA), BF16), jax.ShapeDtypeStruct((B, S, 128), BF16)),
        grid=(B, S // tr),
        in_specs=[pl.BlockSpec((1, tr, 768), lambda b, i: (b, i, 0)),
                  pl.BlockSpec((tr, 256), lambda b, i: (i, 0)),
                  pl.BlockSpec((1, KV_LORA), lambda b, i: (0, 0)),
                  pl.BlockSpec((1, 128), lambda b, i: (0, 0)),
                  pl.BlockSpec((1, 128), lambda b, i: (0, 0))],
        out_specs=(pl.BlockSpec((1, tr, KV_LORA), lambda b, i: (b, i, 0)),
                   pl.BlockSpec((1, tr, 128), lambda b, i: (b, i, 0))),
        compiler_params=_cparams(2), name="mla_small",
    )(p_small, tab, g_kv, g_kr, g_ks)


def _kv_up_body(x_ref, w_ref, kr_ref, gk_ref, k_ref, v_ref, *, hg):
    acc = _bdot(x_ref[0], w_ref[...])
    kr = kr_ref[0]
    for h in range(hg):
        kn = _rms(acc[:, h * 128:(h + 1) * 128], gk_ref[...])
        k_ref[0, :, h * 256:h * 256 + 128] = kn.astype(k_ref.dtype)
        k_ref[0, :, h * 256 + 128:(h + 1) * 256] = kr
    v_ref[0] = acc[:, hg * 128:].astype(v_ref.dtype)


def _kv_up(ckv, w_perm, kr, g_k):
    B, S, _ = ckv.shape
    hg = HEAD_GROUP
    tm = _pick_tile(S, 1152, 8)
    return pl.pallas_call(
        functools.partial(_kv_up_body, hg=hg),
        out_shape=(jax.ShapeDtypeStruct((B, S, MLA_HEADS * 256), BF16),
                   jax.ShapeDtypeStruct((B, S, MLA_HEADS * V_HEAD), BF16)),
        grid=(B, S // tm, MLA_HEADS // hg),
        in_specs=[pl.BlockSpec((1, tm, KV_LORA), lambda b, i, j: (b, i, 0)),
                  pl.BlockSpec((KV_LORA, hg * 256), lambda b, i, j: (0, j)),
                  pl.BlockSpec((1, tm, 128), lambda b, i, j: (b, i, 0)),
                  pl.BlockSpec((1, 128), lambda b, i, j: (0, 0))],
        out_specs=(pl.BlockSpec((1, tm, hg * 256), lambda b, i, j: (b, i, j)),
                   pl.BlockSpec((1, tm, hg * 128), lambda b, i, j: (b, i, j))),
        compiler_params=_cparams(3), name="kv_up",
    )(ckv, w_perm, kr, g_k)


def _q_up_body(x_ref, w_ref, tab_ref, gq_ref, gqa_ref, gr_ref, gs_ref, q_ref, xs_ref, *, hg, scale):
    @pl.when(pl.program_id(2) == 0)
    def _():
        xs_ref[...] = _rms(x_ref[0], gqa_ref[...]).astype(BF16)

    acc = _bdot(xs_ref[...], w_ref[...])
    tab = tab_ref[...]
    for h in range(hg):
        qn = _rms(acc[:, h * 128:(h + 1) * 128], gq_ref[...]) * scale
        xr = acc[:, (hg + h) * 128:(hg + h + 1) * 128]
        xw = acc[:, (2 * hg + h) * 128:(2 * hg + h + 1) * 128]
        inv = lax.rsqrt(jnp.sum(xr * xr, axis=-1, keepdims=True) * (1.0 / QK_ROPE) + EPS) * scale
        qr = inv * (xr * gr_ref[...] * tab[:, :128] + xw * gs_ref[...] * tab[:, 128:])
        q_ref[0, :, h * 256:h * 256 + 128] = qn.astype(q_ref.dtype)
        q_ref[0, :, h * 256 + 128:(h + 1) * 256] = qr.astype(q_ref.dtype)


def _q_up(cq, w_perm, tab, g_q, g_qa, g_r, g_s):
    B, T, _ = cq.shape
    hg = HEAD_GROUP
    tm = _pick_tile(T, 512, 8)
    scale = float((QK_NOPE + QK_ROPE) ** -0.5 * np.log2(np.e))
    return pl.pallas_call(
        functools.partial(_q_up_body, hg=hg, scale=scale),
        out_shape=jax.ShapeDtypeStruct((B, T, MLA_HEADS * 256), BF16),
        grid=(B, T // tm, MLA_HEADS // hg),
        in_specs=[pl.BlockSpec((1, tm, Q_LORA), lambda b, i, j: (b, i, 0)),
                  pl.BlockSpec((Q_LORA, hg * 384), lambda b, i, j: (0, j)),
                  pl.BlockSpec((tm, 256), lambda b, i, j: (i, 0)),
                  pl.BlockSpec((1, 128), lambda b, i, j: (0, 0)),
                  pl.BlockSpec((1, Q_LORA), lambda b, i, j: (0, 0)),
                  pl.BlockSpec((1, 128), lambda b, i, j: (0, 0)),
                  pl.BlockSpec((1, 128), lambda b, i, j: (0, 0))],
        out_specs=pl.BlockSpec((1, tm, hg * 256), lambda b, i, j: (b, i, j)),
        scratch_shapes=[pltpu.VMEM((tm, Q_LORA), BF16)],
        compiler_params=_cparams(3), name="q_up",
    )(cq, w_perm, tab, g_q, g_qa, g_r, g_s)


def _attn_body(q_ref, k_ref, v_ref, o_ref, *, n_split):
    k, v = k_ref[0], v_ref[0]
    rows = q_ref.shape[1] // n_split
    sl = [slice(i * rows, (i + 1) * rows) for i in range(n_split)]
    s = [lax.dot_general(q_ref[0, r, :], k, (((1,), (1,)), ((), ())), preferred_element_type=F32) for r in sl]
    m = [jnp.max(x, axis=-1, keepdims=True) for x in s]
    p = [jnp.exp2(x - mx) for x, mx in zip(s, m)]
    l = [jnp.sum(x, axis=-1, keepdims=True) for x in p]
    o = [_bdot(x.astype(BF16), v) for x in p]
    for r, ox, lx in zip(sl, o, l):
        o_ref[0, r, :] = (ox / lx).astype(o_ref.dtype)


def _attention(q, k, v):
    B, T, _ = q.shape
    S = k.shape[1]
    tq = _pick_tile(T, ATTN_ROWS, 8)
    return pl.pallas_call(
        functools.partial(_attn_body, n_split=ATTN_SPLIT if tq % (8 * ATTN_SPLIT) == 0 else 1),
        out_shape=jax.ShapeDtypeStruct((B, T, MLA_HEADS * V_HEAD), BF16),
        grid=(B, MLA_HEADS, T // tq),
        in_specs=[pl.BlockSpec((1, tq, 256), lambda b, h, i: (b, i, h)),
                  pl.BlockSpec((1, S, 256), lambda b, h, i: (b, 0, h)),
                  pl.BlockSpec((1, S, V_HEAD), lambda b, h, i: (b, 0, h))],
        out_specs=pl.BlockSpec((1, tq, V_HEAD), lambda b, h, i: (b, i, h)),
        compiler_params=_cparams(3), name="mla_attention",
    )(q, k, v)


def _conv_body(prev_ref, cur_ref, next_ref, w_ref, o_ref, buf_ref, *, n_lat, n_tiles, l2, scale, tr):
    j = pl.program_id(1)
    tc = cur_ref.shape[2]
    prev_ok = jnp.logical_and(j != 0, j != n_lat)
    next_ok = jnp.logical_and(j != n_lat - 1, j != n_tiles - 1)
    zeros8 = jnp.zeros((8, tc), F32)
    buf_ref[0:8, :] = jnp.where(prev_ok, prev_ref[0].astype(F32)[CONV_HALO - 8:], zeros8)
    buf_ref[8:8 + tr, :] = cur_ref[0].astype(F32)
    buf_ref[8 + tr:16 + tr, :] = jnp.where(next_ok, next_ref[0].astype(F32)[:8], zeros8)
    w = w_ref[...]
    y = jnp.zeros((tr, tc), F32)
    for t in range(CONV_W):
        r0 = 8 + t - CONV_W // 2
        y = y + buf_ref[r0:r0 + tr, :] * w[t:t + 1, :]
    y = _silu(y)
    if l2:
        for h in range(tc // 128):
            yh = y[:, h * 128:(h + 1) * 128]
            yh = yh * lax.rsqrt(jnp.sum(yh * yh, axis=-1, keepdims=True) + EPS) * scale
            o_ref[0, :, h * 128:(h + 1) * 128] = yh.astype(o_ref.dtype)
    else:
        o_ref[0] = y.astype(o_ref.dtype)


def _short_conv(x, col0, width, w8, *, n_lat_rows, l2, scale=1.0):
    B, R, _ = x.shape
    tr = _pick_tile(int(np.gcd(n_lat_rows, R - n_lat_rows)) if R > n_lat_rows else n_lat_rows, 256, CONV_HALO)
    tc = _pick_tile(width, 1024)
    n_tiles, n_lat = R // tr, n_lat_rows // tr
    cb = col0 // tc
    hpt = tr // CONV_HALO
    n_halo = R // CONV_HALO
    assert col0 % tc == 0
    return pl.pallas_call(
        functools.partial(_conv_body, n_lat=n_lat, n_tiles=n_tiles, l2=l2, scale=scale, tr=tr),
        out_shape=jax.ShapeDtypeStruct((B, R, width), BF16),
        grid=(B, n_tiles, width // tc),
        in_specs=[pl.BlockSpec((1, CONV_HALO, tc), lambda b, j, c: (b, jnp.maximum(j * hpt - 1, 0), cb + c)),
                  pl.BlockSpec((1, tr, tc), lambda b, j, c: (b, j, cb + c)),
                  pl.BlockSpec((1, CONV_HALO, tc),
                               lambda b, j, c: (b, jnp.minimum((j + 1) * hpt, n_halo - 1), cb + c)),
                  pl.BlockSpec((8, tc), lambda b, j, c: (0, c))],
        out_specs=pl.BlockSpec((1, tr, tc), lambda b, j, c: (b, j, c)),
        scratch_shapes=[pltpu.VMEM((tr + 16, tc), F32)],
        compiler_params=_cparams(3), name="short_conv",
    )(x, x, x, w8)


def _dn_consts():
    C = DN_CHUNK
    r = np.arange(C)[:, None]
    c = np.arange(C)[None, :]
    tri = np.stack([(c <= r), (c >= r)]).astype(np.float32)
    ones = np.ones((C, C), np.float32)
    rhs = np.stack([np.concatenate([(r > c).astype(np.float32), ones], 1),
                    np.concatenate([(r < c).astype(np.float32), ones], 1)])
    rhs2 = np.concatenate([rhs, rhs], axis=1)
    incl = np.stack([(r >= c), (r <= c)]).astype(np.float32)
    strict = np.stack([(r > c), (r < c)]).astype(np.float32)
    levels = []
    s = 1
    while s < C:
        levels.append(((r // (2 * s) == c // (2 * s)) & (r // s != c // s)).astype(np.float32))
        s *= 2
    return (jnp.asarray(tri, BF16), jnp.asarray(rhs2, BF16), jnp.asarray(incl), jnp.asarray(strict),
            jnp.asarray(np.stack(levels)))


def _dn_chunks(ch, consts, with_q):
    C = DN_CHUNK
    tri_ref, rhs2_ref, incl_ref, strict_ref, lev_ref = consts
    n = range(len(ch))
    d = [c["d"] for c in ch]
    last = [C - 1 if di == 0 else 0 for di in d]
    la = [-c["a_scale"] * jax.nn.softplus(c["a_row"] + c["dt_b"]) for c in ch]
    beta = [jax.nn.sigmoid(c["b_row"]) for c in ch]
    dg = []
    for i in n:
        hi = la[i].astype(BF16)
        lo = (la[i] - hi.astype(F32)).astype(BF16)
        t = tri_ref[d[i]]
        dg.append(_bdot(jnp.concatenate([t * hi, t * lo], axis=1), rhs2_ref[d[i]]))
    diff = [g[:, :C] for g in dg]
    gcol = [g[:, C:] for g in dg]
    tail_row = [jnp.exp(diff[i][last[i]:last[i] + 1, :]) for i in n]
    g_last = [jnp.exp(gcol[i][last[i]:last[i] + 1, :]) for i in n]
    dec = [incl_ref[d[i]] * jnp.exp(diff[i] * incl_ref[d[i]]) for i in n]
    egc = [jnp.exp(g) for g in gcol]
    gram = [_bdot(c["k"], c["kT"]) for c in ch]
    nd = [strict_ref[d[i]] * gram[i] * dec[i] * beta[i] for i in n]
    eye = incl_ref[0] * incl_ref[1]
    x = [eye - nd[i] * lev_ref[0] for i in n]
    for lv in range(1, lev_ref.shape[0]):
        xb = [xi.astype(BF16) for xi in x]
        m = [(nd[i] * lev_ref[lv]).astype(BF16) for i in n]
        xm = [_bdot(xb[i], m[i]).astype(BF16) for i in n]
        x = [x[i] - _bdot(xm[i], xb[i]) for i in n]
    ke = [(ch[i]["k"].astype(F32) * egc[i]).astype(BF16) for i in n]
    y = [_bdot(x[i].astype(BF16), jnp.concatenate([ch[i]["v"], ke[i]], axis=1)) for i in n]
    s_b = [c["S"].astype(BF16) for c in ch]
    if with_q:
        qg = [(ch[i]["q"].astype(F32) * egc[i]).astype(BF16) for i in n]
        r2 = [_bdot(jnp.concatenate([y[i][:, DN_DV:].astype(BF16), qg[i]], axis=0), s_b[i]) for i in n]
        yks = [r[:C] for r in r2]
        qs = [r[C:] for r in r2]
    else:
        yks = [_bdot(y[i][:, DN_DV:].astype(BF16), s_b[i]) for i in n]
    vt = [(y[i][:, :DN_DV] - yks[i]).astype(BF16) for i in n]
    kt_tail = [(ch[i]["kT"].astype(F32) * (beta[i] * tail_row[i])).astype(BF16) for i in n]
    s_new = [ch[i]["S"] * g_last[i] + _bdot(kt_tail[i], vt[i]) for i in n]
    if not with_q:
        return s_new, [None for _ in n]
    qk = [_bdot(c["q"], c["kT"]) for c in ch]
    p = [(qk[i] * dec[i] * beta[i]).astype(BF16) for i in n]
    o = [qs[i] + _bdot(p[i], vt[i]) for i in n]
    return s_new, o


def _dn_body(alog_ref, dtb_ref, k_ref, kT_ref, v_ref, q_ref, ab_ref, z_ref, g_ref,
             tri_ref, rhs2_ref, incl_ref, strict_ref, lev_ref, o_ref, acc_ref, s_ref, *, hb, n_lat, n_ctx):
    C = DN_CHUNK
    hg = pl.program_id(1)
    consts = (tri_ref, rhs2_ref, incl_ref, strict_ref, lev_ref)
    acc_ref[...] = jnp.zeros(acc_ref.shape, F32)
    s_ref[...] = jnp.zeros(s_ref.shape, F32)

    def run_chunk(cf, cb, with_q):
        ch, where = [], []
        for h in range(hb):
            head = hg * hb + h
            for d, c in ((0, cf), (1, cb)):
                row0 = pl.multiple_of(c * C, C)
                cols = slice(h * 128, (h + 1) * 128)
                ch.append(dict(
                    k=k_ref[0, pl.ds(row0, C), cols], v=v_ref[0, pl.ds(row0, C), cols], kT=kT_ref[0, h, c],
                    q=q_ref[0, pl.ds(row0, C), cols] if with_q else None,
                    a_row=ab_ref[0, h, d, pl.ds(c, 1), :], b_row=ab_ref[0, h, 2 + d, pl.ds(c, 1), :],
                    a_scale=jnp.exp(alog_ref[d, head]), dt_b=dtb_ref[d, head], S=s_ref[2 * h + d], d=d))
                where.append((row0, cols))
        s_new, o = _dn_chunks(ch, consts, with_q)
        for i, (row0, cols) in enumerate(where):
            s_ref[i] = s_new[i]
            if with_q:
                acc_ref[pl.ds(row0, C), cols] += o[i]

    @pl.loop(0, n_ctx)
    def _(n):
        run_chunk(n_lat + n, n_lat + n_ctx - 1 - n, False)

    @pl.loop(0, n_lat)
    def _(n):
        run_chunk(n, n_lat - 1 - n, True)

    rows = 256 if (n_lat * C) % 256 == 0 else C

    @pl.loop(0, (n_lat * C) // rows)
    def _(i):
        r0 = pl.multiple_of(i * rows, rows)
        for h in range(hb):
            cols = slice(h * 128, (h + 1) * 128)
            od = acc_ref[pl.ds(r0, rows), cols]
            zz = z_ref[0, pl.ds(r0, rows), cols].astype(F32)
            o_ref[0, pl.ds(r0, rows), cols] = (_rms(od, g_ref[...]) * zz).astype(o_ref.dtype)


def _deltanet(k, kT, v, q, ab, z, a_log, dt_bias, g_dn):
    B, S, _ = k.shape
    T = q.shape[1]
    C = DN_CHUNK
    assert C == DN_DK == DN_DV == V7X_LANES
    hb = DN_HEADS_PER_STEP
    n_lat, n_ctx = T // C, (S - T) // C
    consts = list(_dn_consts())
    smem = pl.BlockSpec(memory_space=pltpu.SMEM)
    one = pl.Buffered(1)

    def full(a):
        nd = a.ndim
        return pl.BlockSpec(a.shape, lambda b, g, _n=nd: (0,) * _n)

    return pl.pallas_call(
        functools.partial(_dn_body, hb=hb, n_lat=n_lat, n_ctx=n_ctx),
        out_shape=jax.ShapeDtypeStruct((B, T, DN_V), BF16),
        grid=(B, DN_HEADS // hb),
        in_specs=[smem, smem,
                  pl.BlockSpec((1, S, hb * 128), lambda b, g: (b, 0, g), pipeline_mode=one),
                  pl.BlockSpec((1, hb, S // C, 128, C), lambda b, g: (b, g, 0, 0, 0), pipeline_mode=one),
                  pl.BlockSpec((1, S, hb * 128), lambda b, g: (b, 0, g), pipeline_mode=one),
                  pl.BlockSpec((1, T, hb * 128), lambda b, g: (b, 0, g), pipeline_mode=one),
                  pl.BlockSpec((1, hb, 4, S // C, C), lambda b, g: (b, g, 0, 0, 0)),
                  pl.BlockSpec((1, T, hb * 128), lambda b, g: (b, 0, g), pipeline_mode=one),
                  pl.BlockSpec((1, 128), lambda b, g: (0, 0))] + [full(a) for a in consts],
        out_specs=pl.BlockSpec((1, T, hb * 128), lambda b, g: (b, 0, g)),
        scratch_shapes=[pltpu.VMEM((T, hb * 128), F32), pltpu.VMEM((2 * hb, DN_DK, DN_DV), F32)],
        compiler_params=_cparams(2), name="deltanet",
    )(a_log, dt_bias, k, kT, v, q, ab, z, g_dn, *consts)


def _merge_body(a_ref, b_ref, wa_ref, wb_ref, ga_ref, gb_ref, o_ref):
    oa = _bdot(a_ref[0], wa_ref[...])
    ob = _bdot(b_ref[0], wb_ref[...])
    o_ref[0] = (ga_ref[0].astype(F32) * oa + gb_ref[0].astype(F32) * ob).astype(o_ref.dtype)


def _merge(att, ob, w_oa, w_ob, gates):
    B, T, Ka = att.shape
    D = w_oa.shape[1]
    tm = _pick_tile(T, 1024, 8)
    tn = _pick_tile(D, 512)
    nj = D // tn
    return pl.pallas_call(
        _merge_body,
        out_shape=jax.ShapeDtypeStruct((B, T, D), BF16),
        grid=(B, T // tm, nj),
        in_specs=[pl.BlockSpec((1, tm, Ka), lambda b, i, j: (b, i, 0)),
                  pl.BlockSpec((1, tm, Ka), lambda b, i, j: (b, i, 0)),
                  pl.BlockSpec((Ka, tn), lambda b, i, j: (0, j)),
                  pl.BlockSpec((Ka, tn), lambda b, i, j: (0, j)),
                  pl.BlockSpec((1, tm, tn), lambda b, i, j: (b, i, j)),
                  pl.BlockSpec((1, tm, tn), lambda b, i, j: (b, i, j + nj))],
        out_specs=pl.BlockSpec((1, tm, tn), lambda b, i, j: (b, i, j)),
        compiler_params=_cparams(3), name="merge",
    )(att, ob, w_oa, w_ob, gates, gates)


def _norm_router_body(x_ref, g_ref, sh_ref, sc_ref, whl_ref, br_ref, h_ref, id_ref, gate_ref):
    h = _rms(x_ref[0], g_ref[...]) * (1.0 + sc_ref[0]) + sh_ref[0]
    half = h.shape[1] // 2
    h_ref[0] = _pack_bf16_pair(h[:, :half], h[:, half:])
    hh = h.astype(BF16)
    hl = (h - hh.astype(F32)).astype(BF16)
    r = _bdot(hh, whl_ref[...])
    logits = r[:, :128] + r[:, 128:] + _bdot(hl, whl_ref[:, :128]) + br_ref[...]
    lane = lax.broadcasted_iota(jnp.int32, logits.shape, 1)
    neg = jnp.float32(-jnp.inf)
    big = jnp.int32(1 << 20)
    is_g = lane < N_GROUPS
    lg = jnp.where(is_g, logits, neg)
    mg = jnp.max(lg, axis=-1, keepdims=True)
    eg = jnp.where(is_g, jnp.exp(lg - mg), 0.0)
    pg = eg / jnp.sum(eg, axis=-1, keepdims=True)
    pg_top = jnp.max(pg, axis=-1, keepdims=True)
    g_top = jnp.min(jnp.where(jnp.logical_and(is_g, pg == pg_top), lane, big), axis=-1, keepdims=True)
    lo = N_GROUPS + EXPERTS_PER_GROUP * g_top
    is_e = jnp.logical_and(lane >= lo, lane < lo + EXPERTS_PER_GROUP)
    le = jnp.where(is_e, logits, neg)
    me = jnp.max(le, axis=-1, keepdims=True)
    ee = jnp.where(is_e, jnp.exp(le - me), 0.0)
    pe = ee / jnp.sum(ee, axis=-1, keepdims=True)
    p1 = jnp.max(pe, axis=-1, keepdims=True)
    i1 = jnp.min(jnp.where(jnp.logical_and(is_e, pe == p1), lane, big), axis=-1, keepdims=True)
    rest = jnp.logical_and(is_e, lane != i1)
    pe2 = jnp.where(rest, pe, -1.0)
    p2 = jnp.max(pe2, axis=-1, keepdims=True)
    i2 = jnp.min(jnp.where(jnp.logical_and(rest, pe2 == p2), lane, big), axis=-1, keepdims=True)
    den = p1 + p2
    id_ref[0] = jnp.where(lane == 0, i1 - N_GROUPS, jnp.where(lane == 1, i2 - N_GROUPS, 0))
    gate_ref[0] = jnp.where(lane == 0, p1 / den * pg_top, jnp.where(lane == 1, p2 / den * pg_top, 0.0))


def _norm_router(x, g, shift, scale, w_hl, b_r):
    B, T, D = x.shape
    tr = _pick_tile(T, 256, 8)
    return pl.pallas_call(
        _norm_router_body,
        out_shape=(jax.ShapeDtypeStruct((B, T, D // 2), jnp.uint32),
                   jax.ShapeDtypeStruct((B, T, 128), jnp.int32),
                   jax.ShapeDtypeStruct((B, T, 128), F32)),
        grid=(B, T // tr),
        in_specs=[pl.BlockSpec((1, tr, D), lambda b, i: (b, i, 0)),
                  pl.BlockSpec((1, D), lambda b, i: (0, 0)),
                  pl.BlockSpec((1, 1, D), lambda b, i: (b, 0, 0)),
                  pl.BlockSpec((1, 1, D), lambda b, i: (b, 0, 0)),
                  pl.BlockSpec((D, 256), lambda b, i: (0, 0)),
                  pl.BlockSpec((1, 128), lambda b, i: (0, 0))],
        out_specs=(pl.BlockSpec((1, tr, D // 2), lambda b, i: (b, i, 0)),
                   pl.BlockSpec((1, tr, 128), lambda b, i: (b, i, 0)),
                   pl.BlockSpec((1, tr, 128), lambda b, i: (b, i, 0))),
        compiler_params=_cparams(2), name="norm_router",
    )(x, g, shift, scale, w_hl, b_r)


def _row_gather(idx_ref, src_ref, buf_ref, sem, slot, n_groups):
    def row_copy(r, s):
        return pltpu.make_async_copy(src_ref.at[pl.ds(s, 1)], buf_ref.at[slot, pl.ds(r, 1)], sem.at[slot])

    def start():
        def body(g, c):
            for j in range(ROW_DMA_GROUP):
                r = g * ROW_DMA_GROUP + j
                row_copy(r, idx_ref[0, 0, r]).start(priority=j % 2)
            return c
        lax.fori_loop(0, n_groups, body, 0)

    def wait():
        def body(g, c):
            for j in range(ROW_DMA_GROUP):
                row_copy(g * ROW_DMA_GROUP + j, 0).wait()
            return c
        lax.fori_loop(0, n_groups, body, 0)

    return start, wait


def _groups(n_rows):
    return (n_rows + ROW_DMA_GROUP - 1) // ROW_DMA_GROUP


def _moe_gather_body(nv_ref, src_ref, nxt_ref, h_ref, o_ref, buf_ref, sem, *, nb):
    i = pl.program_id(0)
    slot = lax.rem(i, 2)
    nv = nv_ref[i]
    nv_next = jnp.where(i + 1 < nb, nv_ref[jnp.minimum(i + 1, nb - 1)], 0)
    start_cur, wait_cur = _row_gather(src_ref, h_ref, buf_ref, sem, slot, _groups(nv))
    start_nxt, _ = _row_gather(nxt_ref, h_ref, buf_ref, sem, 1 - slot, _groups(nv_next))

    @pl.when(i == 0)
    def _():
        buf_ref[...] = jnp.zeros(buf_ref.shape, buf_ref.dtype)
        start_cur()

    start_nxt()

    @pl.when(nv > 0)
    def _():
        wait_cur()
        lo, hi = _unpack_bf16_pair(buf_ref[slot])
        half = lo.shape[1]
        o_ref[:, :half] = lo.astype(o_ref.dtype)
        o_ref[:, half:] = hi.astype(o_ref.dtype)

    @pl.when(nv == 0)
    def _():
        o_ref[...] = jnp.zeros(o_ref.shape, o_ref.dtype)


def _moe_gather(h, src, n_valid):
    n, D = h.shape[0], 2 * h.shape[1]
    R = src.shape[0]
    nb = R // MOE_ROWS
    src3 = src.reshape(nb, 1, MOE_ROWS)
    return pl.pallas_call(
        functools.partial(_moe_gather_body, nb=nb),
        out_shape=jax.ShapeDtypeStruct((R, D), BF16),
        grid_spec=pltpu.PrefetchScalarGridSpec(
            num_scalar_prefetch=1, grid=(nb,),
            in_specs=[pl.BlockSpec((1, 1, MOE_ROWS), lambda i, nv: (i, 0, 0), memory_space=pltpu.SMEM),
                      pl.BlockSpec((1, 1, MOE_ROWS), lambda i, nv: (jnp.minimum(i + 1, nb - 1), 0, 0),
                                   memory_space=pltpu.SMEM),
                      pl.BlockSpec(memory_space=pl.ANY)],
            out_specs=pl.BlockSpec((MOE_ROWS, D), lambda i, nv: (i, 0)),
            scratch_shapes=[pltpu.VMEM((2, MOE_ROWS, D // 2), jnp.uint32), pltpu.SemaphoreType.DMA((2,))]),
        compiler_params=_cparams(1), name="moe_gather",
    )(n_valid, src3, src3, h)


def _moe_used(nv, o_ref, compute):
    @pl.when(nv > 0)
    def _():
        compute()

    @pl.when(nv == 0)
    def _():
        o_ref[...] = jnp.zeros(o_ref.shape, o_ref.dtype)


def _moe_up_body(be_ref, nv_ref, x_ref, w1_ref, w3_ref, o_ref, w1b_ref, w3b_ref):
    i = pl.program_id(1)
    changed = jnp.logical_or(i == 0, be_ref[i] != be_ref[jnp.maximum(i - 1, 0)])

    @pl.when(changed)
    def _():
        w1b_ref[...] = w1_ref[0].astype(BF16)
        w3b_ref[...] = w3_ref[0].astype(BF16)

    def compute():
        x = x_ref[...]
        a = _bdot(x, w1b_ref[...])
        b = _bdot(x, w3b_ref[...])
        o_ref[...] = (_silu(a) * b).astype(o_ref.dtype)

    _moe_used(nv_ref[i], o_ref, compute)


def _moe_down_body(be_ref, nv_ref, h_ref, w2_ref, o_ref, w2b_ref):
    i = pl.program_id(1)
    changed = jnp.logical_or(i == 0, be_ref[i] != be_ref[jnp.maximum(i - 1, 0)])

    @pl.when(changed)
    def _():
        w2b_ref[...] = w2_ref[0].astype(BF16)

    def compute():
        y = _bdot(h_ref[...], w2b_ref[...])
        half = y.shape[1] // 2
        o_ref[...] = _pack_bf16_pair(y[:, :half], y[:, half:])

    _moe_used(nv_ref[i], o_ref, compute)


def _moe_experts(x_pad, blk_expert, n_valid, w1, w3, w2):
    R, D = x_pad.shape
    nb = R // MOE_ROWS
    Hd = w1.shape[2]
    th = _pick_tile(Hd, 512)
    hbuf = pl.pallas_call(
        _moe_up_body,
        out_shape=jax.ShapeDtypeStruct((R, Hd), BF16),
        grid_spec=pltpu.PrefetchScalarGridSpec(
            num_scalar_prefetch=2, grid=(Hd // th, nb),
            in_specs=[pl.BlockSpec((MOE_ROWS, D), lambda j, i, be, nv: (i, 0)),
                      pl.BlockSpec((1, D, th), lambda j, i, be, nv: (be[i], 0, j)),
                      pl.BlockSpec((1, D, th), lambda j, i, be, nv: (be[i], 0, j))],
            out_specs=pl.BlockSpec((MOE_ROWS, th), lambda j, i, be, nv: (i, j)),
            scratch_shapes=[pltpu.VMEM((D, th), BF16), pltpu.VMEM((D, th), BF16)]),
        compiler_params=_cparams(2), name="moe_up",
    )(blk_expert, n_valid, x_pad, w1, w3)
    tn = _pick_tile(D, MOE_DOWN_COLS)
    return pl.pallas_call(
        _moe_down_body,
        out_shape=jax.ShapeDtypeStruct((R, D // 2), jnp.uint32),
        grid_spec=pltpu.PrefetchScalarGridSpec(
            num_scalar_prefetch=2, grid=(D // tn, nb),
            in_specs=[pl.BlockSpec((MOE_ROWS, Hd), lambda j, i, be, nv: (i, 0)),
                      pl.BlockSpec((1, Hd, tn), lambda j, i, be, nv: (be[i], 0, j))],
            out_specs=pl.BlockSpec((MOE_ROWS, tn // 2), lambda j, i, be, nv: (i, j)),
            scratch_shapes=[pltpu.VMEM((Hd, tn), BF16)]),
        compiler_params=_cparams(2), name="moe_down",
    )(blk_expert, n_valid, hbuf, w2)


def _combine_body(dest_ref, nxt_ref, x_ref, p_ref, g_ref, y_ref, o_ref, buf_ref, sem, *, tr, n_steps, tn):
    t = pl.program_id(0)
    slot = lax.rem(t, 2)
    start_cur, wait_cur = _row_gather(dest_ref, y_ref, buf_ref, sem, slot, TOP_K * tr // ROW_DMA_GROUP)
    start_nxt, _ = _row_gather(nxt_ref, y_ref, buf_ref, sem, 1 - slot, TOP_K * tr // ROW_DMA_GROUP)

    @pl.when(t == 0)
    def _():
        start_cur()

    @pl.when(t + 1 < n_steps)
    def _():
        start_nxt()

    wait_cur()
    p = p_ref[0]
    half = tn // 2
    for j in range(x_ref.shape[2] // tn):
        words = slice(j * half, (j + 1) * half)
        lo0, hi0 = _unpack_bf16_pair(buf_ref[slot, 0:tr, words])
        lo1, hi1 = _unpack_bf16_pair(buf_ref[slot, tr:2 * tr, words])
        for y0, y1, c0 in ((lo0, lo1, j * tn), (hi0, hi1, j * tn + half)):
            cols = slice(c0, c0 + half)
            ffn = p[:, 0:1] * y0 + p[:, 1:2] * y1
            o_ref[0, :, cols] = x_ref[0, :, cols] + g_ref[0, :, cols] * ffn


def _combine(x, y_pad, dest, gate, g2):
    B, T, D = x.shape
    tr = _pick_tile(T, 256, 8)
    nt = T // tr
    n_steps = B * nt
    dest_t = dest.reshape(n_steps, tr, TOP_K).transpose(0, 2, 1).reshape(n_steps, 1, TOP_K * tr)
    row = pl.BlockSpec((1, tr, D), lambda t: (t // nt, t % nt, 0))
    return pl.pallas_call(
        functools.partial(_combine_body, tr=tr, n_steps=n_steps, tn=_pick_tile(D, MOE_DOWN_COLS)),
        out_shape=jax.ShapeDtypeStruct((B, T, D), F32),
        grid=(n_steps,),
        in_specs=[pl.BlockSpec((1, 1, TOP_K * tr), lambda t: (t, 0, 0), memory_space=pltpu.SMEM),
                  pl.BlockSpec((1, 1, TOP_K * tr), lambda t: (jnp.minimum(t + 1, n_steps - 1), 0, 0),
                               memory_space=pltpu.SMEM),
                  row,
                  pl.BlockSpec((1, tr, 128), lambda t: (t // nt, t % nt, 0)),
                  pl.BlockSpec((1, 1, D), lambda t: (t // nt, 0, 0)),
                  pl.BlockSpec(memory_space=pl.ANY)],
        out_specs=row,
        scratch_shapes=[pltpu.VMEM((2, TOP_K * tr, D // 2), jnp.uint32), pltpu.SemaphoreType.DMA((2,))],
        compiler_params=_cparams(1), name="moe_combine",
    )(dest_t, dest_t, x, gate, g2, y_pad)


def _rope_tables(T, Tc):
    rows = T // GRID_W
    row = jnp.repeat(jnp.arange(rows), GRID_W).astype(F32)
    col = jnp.tile(jnp.arange(GRID_W), rows).astype(F32)
    half = QK_ROPE // 2
    inv_freq = ROPE_THETA ** (-jnp.arange(0, half, 2, dtype=F32) / half)
    ang = jnp.concatenate([row[:, None] * inv_freq, col[:, None] * inv_freq], axis=-1)
    cos, sin = jnp.cos(ang), jnp.sin(ang)
    z64 = jnp.zeros((T, 64), F32)
    lat = jnp.concatenate([cos, cos, z64, -sin, sin, z64], axis=-1)
    ctx = jnp.concatenate([jnp.ones((Tc, 64), F32), jnp.zeros((Tc, 192), F32)], axis=-1)
    return jnp.concatenate([lat, ctx], axis=0)


_PERM = np.concatenate([np.arange(0, QK_ROPE, 2), np.arange(1, QK_ROPE, 2)])
_PERM_SW = np.concatenate([np.arange(1, QK_ROPE, 2), np.arange(0, QK_ROPE, 2)])


def _pad_cols(a, n):
    return jnp.pad(a, ((0, 0), (0, n - a.shape[1])))


def _moe_dispatch(expert_ids, n_tokens):
    a = n_tokens * TOP_K
    e_flat = expert_ids.reshape(a)
    onehot = (e_flat[:, None] == jnp.arange(N_EXPERTS)[None, :]).astype(F32)
    ch = 128 if a % 128 == 0 else a
    nc = a // ch
    hp = lax.Precision.HIGHEST
    within = jnp.einsum("ij,cje->cie", jnp.tril(jnp.ones((ch, ch), F32)), onehot.reshape(nc, ch, N_EXPERTS),
                        precision=hp)
    before = jnp.einsum("ij,je->ie", jnp.tril(jnp.ones((nc, nc), F32), -1), within[:, -1, :], precision=hp)
    csum = (within + before[:, None, :]).reshape(a, N_EXPERTS)
    rank = jnp.sum(csum * onehot, axis=1).astype(jnp.int32) - 1
    counts = csum[-1].astype(jnp.int32)
    padded = (counts + MOE_ROWS - 1) // MOE_ROWS * MOE_ROWS
    pad_end = jnp.cumsum(padded)
    pad_start = pad_end - padded
    dest = pad_start[e_flat] + rank
    n_blocks = -(-a // MOE_ROWS) + N_EXPERTS
    src = jnp.zeros((n_blocks * MOE_ROWS,), jnp.int32).at[dest].set(jnp.arange(a, dtype=jnp.int32) // TOP_K)
    blk_expert = jnp.minimum(
        jnp.searchsorted(pad_end, jnp.arange(n_blocks) * MOE_ROWS, side="right"), N_EXPERTS - 1).astype(jnp.int32)
    n_valid = jnp.clip(counts[blk_expert] - (jnp.arange(n_blocks) * MOE_ROWS - pad_start[blk_expert]), 0, MOE_ROWS)
    return dest, src, blk_expert, n_valid.astype(jnp.int32)


def _layer(x, c, ctx, c_ctx, p):
    B, T, D = x.shape
    Tc = ctx.shape[1]
    S = T + Tc
    H = MLA_HEADS

    cin = jnp.zeros((1, 16, D), F32).at[0, :B].set(c).at[0, B].set(c_ctx)
    mod = _matmul(cin, p["w_mod"], tm=16, tn=1024, out_dtype=F32, pre=_silu,
                  epi=lambda acc, b: acc + b, extras=[(p["b_mod"].reshape(1, 1, -1), "bcol")], name="mod")[0]
    mod_l = mod[:B].reshape(B, N_MOD, 1, D)
    mod_c = mod[B].reshape(N_MOD, 1, 1, D)

    h = _norm_mod(x, ctx, p["norm1_g"].reshape(1, D), mod_l[:, 0], mod_l[:, 1], mod_c[0], mod_c[1])

    w_in = p["w_in"]
    wkr = w_in[:, OFF_KR:OFF_AB]
    w_small = jnp.concatenate([w_in[:, OFF_CKV:OFF_KR], _pad_cols(wkr[:, _PERM], 128),
                               _pad_cols(wkr[:, _PERM_SW], 128), _pad_cols(w_in[:, OFF_AB:OFF_DK], 128)],
                              axis=1).astype(BF16)
    p_small = _matmul(h, w_small, tm=1152, tn=896, out_dtype=F32, name="in_small")
    p_dkv = _matmul(h, w_in[:, OFF_DK:OFF_DQ].astype(BF16), tm=1152, name="in_dkv")
    p_dq = _matmul(h, w_in[:, OFF_DQ:OFF_CQ].astype(BF16), rows=T, name="in_dq")
    p_cq = _matmul(h, w_in[:, OFF_CQ:OFF_Z].astype(BF16), rows=T, out_dtype=F32, name="in_cq")
    z_act = _matmul(h, w_in[:, OFF_Z:OFF_GATE].astype(BF16), rows=T, epi=_silu, name="in_z")
    gates = _matmul(h, w_in[:, OFF_GATE:].astype(BF16), rows=T, epi=jax.nn.sigmoid, name="in_gate")

    tab = _rope_tables(T, Tc)
    gkr = p["k_rope_norm_g"]
    ckv_n, k_rope = _mla_small(p_small, tab, p["kv_norm_g"].reshape(1, -1),
                               _pad_cols(gkr[_PERM][None], 128), _pad_cols(gkr[_PERM_SW][None], 128))
    w_ukv = p["w_ukv"].reshape(KV_LORA, H, QK_NOPE + V_HEAD)
    hg = HEAD_GROUP
    w_kv_perm = jnp.concatenate(
        [w_ukv[:, :, :QK_NOPE].reshape(KV_LORA, H // hg, hg * QK_NOPE),
         w_ukv[:, :, QK_NOPE:].reshape(KV_LORA, H // hg, hg * V_HEAD)], axis=2).reshape(KV_LORA, H * 256).astype(BF16)
    k_all, v_all = _kv_up(ckv_n, w_kv_perm, k_rope, p["k_norm_g"].reshape(1, -1))

    w_uq = p["w_uq"].reshape(Q_LORA, H, QK_NOPE + QK_ROPE)
    wq_r = w_uq[:, :, QK_NOPE:]
    zpad = jnp.zeros((Q_LORA, H, 64), F32)
    w_q_perm = jnp.concatenate(
        [w_uq[:, :, :QK_NOPE].reshape(Q_LORA, H // hg, hg * 128),
         jnp.concatenate([wq_r[:, :, _PERM], zpad], axis=2).reshape(Q_LORA, H // hg, hg * 128),
         jnp.concatenate([wq_r[:, :, _PERM_SW], zpad], axis=2).reshape(Q_LORA, H // hg, hg * 128)],
        axis=2).reshape(Q_LORA, H * 384).astype(BF16)
    gqr = p["q_rope_norm_g"]
    q_all = _q_up(p_cq, w_q_perm, tab[:T], p["q_norm_g"].reshape(1, -1), p["q_a_norm_g"].reshape(1, -1),
                  _pad_cols(gqr[_PERM][None], 128), _pad_cols(gqr[_PERM_SW][None], 128))
    att = _attention(q_all, k_all, v_all)

    cw = jnp.pad(p["conv_w"], ((0, 8 - CONV_W), (0, 0)))
    dn_k = _short_conv(p_dkv, 0, DN_QK, cw[:, :DN_QK], n_lat_rows=T, l2=True)
    dn_v = _short_conv(p_dkv, DN_QK, DN_V, cw[:, DN_QK:DN_QK + DN_V], n_lat_rows=T, l2=False)
    dn_q = _short_conv(p_dq, 0, DN_QK, cw[:, DN_QK + DN_V:], n_lat_rows=T, l2=True, scale=float(DN_DK ** -0.5))
    C = DN_CHUNK
    dn_kT = dn_k.reshape(B, S // C, C, DN_HEADS, DN_DK).transpose(0, 3, 1, 4, 2)
    ab = p_small[:, :, 768:768 + 4 * DN_HEADS].reshape(B, S // C, C, 4, DN_HEADS).transpose(0, 4, 3, 1, 2)
    ob_in = _deltanet(dn_k, dn_kT, dn_v, dn_q, ab, z_act, p["a_log"], p["dt_bias"], p["dn_norm_g"].reshape(1, -1))

    mix = _merge(att, ob_in, p["w_oa"].astype(BF16), p["w_ob"].astype(BF16), gates)
    xl = _matmul(mix, p["w_out"].astype(BF16), out_dtype=F32,
                 epi=lambda acc, xr, g: xr + g * acc, extras=[(x, "tile"), (mod_l[:, 2], "bcol")], name="out_proj")

    w_r = _pad_cols(jnp.concatenate([p["w_rg"], p["w_re"]], axis=1), 128)
    w_r_hi = w_r.astype(BF16)
    w_r_lo = (w_r - w_r_hi.astype(F32)).astype(BF16)
    b_r = _pad_cols(jnp.concatenate([p["b_rg"], p["b_re"]])[None], 128)
    h2, ids, gate = _norm_router(xl, p["norm2_g"].reshape(1, D), mod_l[:, 3], mod_l[:, 4],
                                 jnp.concatenate([w_r_hi, w_r_lo], axis=1), b_r)
    n = B * T
    dest, src, blk_expert, n_valid = _moe_dispatch(ids[:, :, :TOP_K].reshape(n, TOP_K), n)
    x_pad = _moe_gather(h2.reshape(n, D // 2), src, n_valid)
    y_pad = _moe_experts(x_pad, blk_expert, n_valid, p["w1"], p["w3"], p["w2"])
    return _combine(xl, y_pad, dest.reshape(n, TOP_K), gate, mod_l[:, 5])


def kernel(x, c, ctx, c_ctx, norm1_g, norm2_g, w_mod, b_mod, w_in, q_a_norm_g, w_uq, kv_norm_g, w_ukv, q_norm_g,
           q_rope_norm_g, k_norm_g, k_rope_norm_g, conv_w, a_log, dt_bias, dn_norm_g, w_oa, w_ob, w_out, w_rg, b_rg,
           w_re, b_re, w1, w3, w2):
    depth = norm1_g.shape[0]
    assert depth == 1, "single-layer block: the context stream is read, never updated"
    layer = 0
    p = {
        "norm1_g": norm1_g[layer], "norm2_g": norm2_g[layer], "w_mod": w_mod[layer], "b_mod": b_mod[layer],
        "w_in": w_in[layer], "q_a_norm_g": q_a_norm_g[layer], "w_uq": w_uq[layer], "kv_norm_g": kv_norm_g[layer],
        "w_ukv": w_ukv[layer], "q_norm_g": q_norm_g[layer], "q_rope_norm_g": q_rope_norm_g[layer],
        "k_norm_g": k_norm_g[layer], "k_rope_norm_g": k_rope_norm_g[layer], "conv_w": conv_w[layer],
        "a_log": a_log[layer], "dt_bias": dt_bias[layer], "dn_norm_g": dn_norm_g[layer], "w_oa": w_oa[layer],
        "w_ob": w_ob[layer], "w_out": w_out[layer], "w_rg": w_rg[layer], "b_rg": b_rg[layer], "w_re": w_re[layer],
        "b_re": b_re[layer], "w1": w1[layer], "w3": w3[layer], "w2": w2[layer],
    }
    return _layer(x, c, ctx, c_ctx, p)
```
